```python
import math
import numpy as np
import jax
import jax.numpy as jnp
from jax import lax

D_MODEL = 2048
BATCH = 2
SEQ = 8192
DEPTH = 1
DEC_BATCH = 4
DEC_SEQ = 8192
PAST_LEN = 128

N_MEM = 256
HG_HEADS = 8
HG_DK = 128
HG_DV = 128
GD_HEADS = 8
GD_DK = 128
GD_DV = 128
MIX_WIDTH = HG_HEADS * HG_DV + GD_HEADS * GD_DV
IN_COLS = 3 * HG_HEADS * HG_DK + 2 * HG_HEADS * HG_DV + 2 * GD_HEADS * GD_DK + 2 * GD_HEADS * GD_DV + 4 * GD_HEADS
CONV_K = 5
CHUNK = 64
XA_HEADS = 4
XA_HEAD_DIM = D_MODEL // XA_HEADS
N_GROUPS = 4
EXPERTS_PER_GROUP = 8
N_EXPERTS = N_GROUPS * EXPERTS_PER_GROUP
TOP_K_IN_GROUP = 2
D_EXPERT = 512
MOE_BLOCK = 128
DN_ALPHA = (2.0 * DEPTH) ** 0.25
DN_BETA = (8.0 * DEPTH) ** -0.25
LN_EPS = 1e-5
NORM_EPS = 1e-6
F32 = jnp.float32

kernel_name = 'hybrid_hgrn2_gdn_hmoe_encoder'


def _layernorm(x, g, b):
    xf = x.astype(F32)
    mu = jnp.mean(xf, axis=-1, keepdims=True)
    var = jnp.mean(jnp.square(xf - mu), axis=-1, keepdims=True)
    return ((xf - mu) * lax.rsqrt(var + LN_EPS) * g.astype(F32) + b.astype(F32)).astype(x.dtype)


def _rmsnorm(a, w):
    return a * lax.rsqrt(jnp.mean(jnp.square(a), axis=-1, keepdims=True) + NORM_EPS) * w.astype(F32)


def _l2norm(a):
    return a * lax.rsqrt(jnp.sum(jnp.square(a), axis=-1, keepdims=True) + NORM_EPS)


def _heads(a, n_heads):
    b, t, _ = a.shape
    return a.reshape(b, t, n_heads, -1).transpose(0, 2, 1, 3)


def _merge_heads(a):
    b, h, t, d = a.shape
    return a.transpose(0, 2, 1, 3).reshape(b, t, h * d)


def _flip_t(a):
    return jnp.flip(a, axis=2)


def _blocks(a):
    b, h, t = a.shape[:3]
    return a.reshape(b, h, t // CHUNK, CHUNK, *a.shape[3:])


def _unblocks(a):
    a = jnp.moveaxis(a, 0, 2)
    b, h, n, c = a.shape[:4]
    return a.reshape(b, h, n * c, *a.shape[4:])


def _split_columns(proj):
    sizes = ([HG_HEADS * HG_DK] * 3 + [HG_HEADS * HG_DV] * 2
             + [GD_HEADS * GD_DK] * 2 + [GD_HEADS * GD_DV] * 2 + [GD_HEADS] * 4)
    return jnp.split(proj, np.cumsum(sizes)[:-1].tolist(), axis=-1)


def hgrn2_chunked(q, k, v, log_f):
    b, h, _, dk = q.shape
    dv = v.shape[-1]
    q = q * dk ** -0.5
    tri = jnp.tril(jnp.ones((CHUNK, CHUNK), bool))

    def step(state, xs):
        qc, kc, vc, gc = xs
        gcum = jnp.cumsum(gc, axis=2)
        diff = gcum[:, :, :, None, :] - gcum[:, :, None, :, :]
        dec = jnp.exp(jnp.where(tri[:, :, None], diff, -jnp.inf))
        scores = jnp.einsum('bhtk,bhtsk,bhsk->bhts', qc, dec, kc)
        o = jnp.einsum('bhts,bhsv->bhtv', scores, vc) + jnp.einsum('bhtk,bhkv->bhtv', qc * jnp.exp(gcum), state)
        g_last = gcum[:, :, -1:, :]
        state = state * jnp.exp(g_last[:, :, 0, :, None]) + jnp.einsum(
            'bhsk,bhsv->bhkv', kc * jnp.exp(g_last - gcum), vc)
        return state, o

    xs = tuple(jnp.moveaxis(_blocks(a), 2, 0) for a in (q, k, v, log_f))
    _, o = lax.scan(step, jnp.zeros((b, h, dk, dv), F32), xs)
    return _unblocks(o)


def gated_delta_chunked(q, k, v, g, beta):
    b, h, _, dk = q.shape
    dv = v.shape[-1]
    q = q * dk ** -0.5
    q, k, v, g, beta = _blocks(q), _blocks(k), _blocks(v), _blocks(g), _blocks(beta)
    gcum = jnp.cumsum(g, axis=-1)
    diff = gcum[..., :, None] - gcum[..., None, :]
    tri_incl = jnp.tril(jnp.ones((CHUNK, CHUNK), bool))
    tri_strict = jnp.tril(jnp.ones((CHUNK, CHUNK), bool), -1)
    dec = jnp.exp(jnp.where(tri_incl, diff, -jnp.inf))
    kb = k * beta[..., None]
    a_strict = jnp.where(tri_strict, jnp.einsum('bhnck,bhnsk->bhncs', kb, k) * dec, 0.0)
    eye = jnp.eye(CHUNK, dtype=F32)
    t_inv = lax.linalg.triangular_solve(eye + a_strict, jnp.broadcast_to(eye, a_strict.shape),
                                        left_side=True, lower=True, unit_diagonal=True)
    u = jnp.matmul(t_inv, v * beta[..., None])
    w = jnp.matmul(t_inv, kb * jnp.exp(gcum)[..., None])
    qk = jnp.einsum('bhnck,bhnsk->bhncs', q, k) * dec

    def step(state, xs):
        qc, kc, uc, wc, qkc, gc = xs
        v_new = uc - jnp.matmul(wc, state)
        o = jnp.matmul(qc * jnp.exp(gc)[..., None], state) + jnp.matmul(qkc, v_new)
        g_last = gc[..., -1]
        state = state * jnp.exp(g_last)[..., None, None] + jnp.einsum(
            'bhck,bhcv->bhkv', kc * jnp.exp(g_last[..., None] - gc)[..., None], v_new)
        return state, o

    xs = tuple(jnp.moveaxis(a, 2, 0) for a in (q, k, u, w, qk, gcum))
    _, o = lax.scan(step, jnp.zeros((b, h, dk, dv), F32), xs)
    return _unblocks(o)


def _centred_depthwise_conv(u, w):
    pad = CONV_K // 2
    return lax.conv_general_dilated(u, w, window_strides=(1,), padding=[(pad, pad)],
                                    dimension_numbers=('NWC', 'WIO', 'NWC'),
                                    feature_group_count=u.shape[-1])


def parallel_mixer(x, layer, w_in, hgrn_lb, hgrn_norm_w, gdn_conv_w, gdn_a_log, gdn_dt_bias,
                   gdn_norm_w, w_out):
    proj = jnp.matmul(x, w_in[layer]).astype(F32)
    (hq, hf_fwd, hf_bwd, hi, hg, gq, gk, gv, gz, ga_fwd, ga_bwd, gb_fwd, gb_bwd) = _split_columns(proj)

    lb = jnp.cumsum(jax.nn.softmax(hgrn_lb.astype(F32), axis=0), axis=0)[layer]

    def hgrn_gates(f_raw, lb_d):
        log_f = jnp.log(lb_d + (1.0 - lb_d) * jax.nn.sigmoid(f_raw))
        k = (1.0 - lb_d) * jax.nn.sigmoid(-f_raw)
        return _heads(k, HG_HEADS), _heads(log_f, HG_HEADS)

    q_h = _heads(hq, HG_HEADS)
    v_h = _heads(hi, HG_HEADS)
    k_f, lf_f = hgrn_gates(hf_fwd, lb[0])
    k_b, lf_b = hgrn_gates(hf_bwd, lb[1])
    o_h = hgrn2_chunked(q_h, k_f, v_h, lf_f) + _flip_t(
        hgrn2_chunked(_flip_t(q_h), _flip_t(k_b), _flip_t(v_h), _flip_t(lf_b)))
    o_h = _rmsnorm(o_h, hgrn_norm_w[layer]) * jax.nn.silu(_heads(hg, HG_HEADS))

    qkv = jnp.concatenate([gq, gk, gv], axis=-1)
    qkv = jax.nn.silu(_centred_depthwise_conv(qkv, gdn_conv_w[layer].astype(F32)))
    cq, ck, cv = jnp.split(qkv, [GD_HEADS * GD_DK, 2 * GD_HEADS * GD_DK], axis=-1)
    q_g = _l2norm(_heads(cq, GD_HEADS))
    k_g = _l2norm(_heads(ck, GD_HEADS))
    v_g = _heads(cv, GD_HEADS)
    a_log = gdn_a_log[layer].astype(F32)
    dt_bias = gdn_dt_bias[layer].astype(F32)

    def log_decay(a_raw, d):
        return (-jnp.exp(a_log[d]) * jax.nn.softplus(a_raw + dt_bias[d])).transpose(0, 2, 1)

    g_f, g_b = log_decay(ga_fwd, 0), log_decay(ga_bwd, 1)
    beta_f = jax.nn.sigmoid(gb_fwd).transpose(0, 2, 1)
    beta_b = jax.nn.sigmoid(gb_bwd).transpose(0, 2, 1)
    o_g = gated_delta_chunked(q_g, k_g, v_g, g_f, beta_f) + _flip_t(
        gated_delta_chunked(_flip_t(q_g), _flip_t(k_g), _flip_t(v_g), _flip_t(g_b), _flip_t(beta_b)))
    o_g = _rmsnorm(o_g, gdn_norm_w[layer]) * jax.nn.silu(_heads(gz, GD_HEADS))

    o = jnp.concatenate([_merge_heads(o_h), _merge_heads(o_g)], axis=-1).astype(x.dtype)
    return jnp.matmul(o, w_out[layer])


def cross_attention(x, mem, w_q, w_kv, w_o):
    b, t, d = x.shape
    m = mem.shape[1]
    q = jnp.matmul(x, w_q).reshape(b, t, XA_HEADS, XA_HEAD_DIM).astype(F32)
    k, v = jnp.split(jnp.matmul(mem, w_kv), 2, axis=-1)
    k = k.reshape(b, m, XA_HEADS, XA_HEAD_DIM).astype(F32)
    v = v.reshape(b, m, XA_HEADS, XA_HEAD_DIM)
    s = jnp.einsum('bthd,bmhd->bhtm', q, k) * XA_HEAD_DIM ** -0.5
    p = jax.nn.softmax(s, axis=-1).astype(v.dtype)
    o = jnp.einsum('bhtm,bmhd->bthd', p, v).reshape(b, t, d)
    return jnp.matmul(o, w_o)


def hierarchical_moe(x2d, w_group, b_group, w_expert, b_expert, w_gate, w_up, w_down):
    n_tok, d = x2d.shape
    xf = x2d.astype(F32)
    g_logits = jnp.matmul(xf, w_group.astype(F32)) + b_group.astype(F32)
    g_sel = jnp.argmax(g_logits, axis=-1).astype(jnp.int32)
    p_group = jnp.take_along_axis(jax.nn.softmax(g_logits, axis=-1), g_sel[:, None], axis=-1)
    e_logits = (jnp.matmul(xf, w_expert.astype(F32)) + b_expert.astype(F32)).reshape(
        n_tok, N_GROUPS, EXPERTS_PER_GROUP)
    e_logits = jnp.take_along_axis(e_logits, g_sel[:, None, None], axis=1)[:, 0]
    p_in, e_loc = lax.top_k(jax.nn.softmax(e_logits, axis=-1), TOP_K_IN_GROUP)
    gate = p_group * p_in / jnp.sum(p_in, axis=-1, keepdims=True)
    expert_id = g_sel[:, None] * EXPERTS_PER_GROUP + e_loc.astype(jnp.int32)

    n_assign = n_tok * TOP_K_IN_GROUP
    flat_e = expert_id.reshape(-1)
    flat_tok = jnp.arange(n_assign, dtype=jnp.int32) // TOP_K_IN_GROUP
    flat_w = gate.reshape(-1)
    order = jnp.argsort(flat_e)
    sorted_e = flat_e[order]
    counts = jnp.zeros((N_EXPERTS,), jnp.int32).at[flat_e].add(1)
    padded = (counts + MOE_BLOCK - 1) // MOE_BLOCK * MOE_BLOCK
    pad_end = jnp.cumsum(padded)
    pad_start = pad_end - padded
    start = jnp.cumsum(counts) - counts
    dest = pad_start[sorted_e] + jnp.arange(n_assign, dtype=jnp.int32) - start[sorted_e]
    n_blocks = -(-n_assign // MOE_BLOCK) + N_EXPERTS
    cap = n_blocks * MOE_BLOCK
    slot_tok = jnp.full((cap,), n_tok, jnp.int32).at[dest].set(flat_tok[order])
    slot_w = jnp.zeros((cap,), F32).at[dest].set(flat_w[order])
    block_e = jnp.minimum(jnp.searchsorted(pad_end, jnp.arange(n_blocks, dtype=jnp.int32) * MOE_BLOCK,
                                           side='right'), N_EXPERTS - 1)
    x_pad = jnp.concatenate([x2d, jnp.zeros((1, d), x2d.dtype)], axis=0)

    def expert_block(args):
        tok, e = args
        xb = x_pad[tok]
        hdn = jax.nn.silu(jnp.matmul(xb, w_gate[e])) * jnp.matmul(xb, w_up[e])
        return jnp.matmul(hdn, w_down[e])

    y_slots = lax.map(expert_block, (slot_tok.reshape(n_blocks, MOE_BLOCK), block_e)).reshape(cap, d)
    y = jax.ops.segment_sum(y_slots.astype(F32) * slot_w[:, None], slot_tok, num_segments=n_tok + 1)[:n_tok]
    return y.astype(x2d.dtype)


def _trunk(x, mem, w_in, hgrn_lb, hgrn_norm_w, gdn_conv_w, gdn_a_log, gdn_dt_bias, gdn_norm_w, w_out,
           ln1_g, ln1_b, xa_w_q, xa_w_kv, xa_w_o, ln2_g, ln2_b, moe_w_group, moe_b_group,
           moe_w_expert, moe_b_expert, moe_w_gate, moe_w_up, moe_w_down, ln3_g, ln3_b):
    for layer in range(DEPTH):
        h = parallel_mixer(x, layer, w_in, hgrn_lb, hgrn_norm_w, gdn_conv_w, gdn_a_log, gdn_dt_bias,
                           gdn_norm_w, w_out)
        x = _layernorm(DN_ALPHA * x + h, ln1_g[layer], ln1_b[layer])
        h = cross_attention(x, mem, xa_w_q[layer], xa_w_kv[layer], xa_w_o[layer])
        x = _layernorm(DN_ALPHA * x + h, ln2_g[layer], ln2_b[layer])
        b, t, d = x.shape
        h = hierarchical_moe(x.reshape(b * t, d), moe_w_group[layer], moe_b_group[layer],
                             moe_w_expert[layer], moe_b_expert[layer], moe_w_gate[layer],
                             moe_w_up[layer], moe_w_down[layer]).reshape(b, t, d)
        x = _layernorm(DN_ALPHA * x + h, ln3_g[layer], ln3_b[layer])
    return x


def setup_inputs(seed: int = 0) -> dict:
    key = jax.random.key(seed)
    ks = jax.random.split(key, 32)

    def normal(k, shape, scale=1.0):
        return jax.random.normal(k, shape, F32) * scale

    def dense(k, shape, fan_in, gain=1.0):
        return normal(k, shape, gain * fan_in ** -0.5)

    dt = jnp.exp(jax.random.uniform(ks[9], (DEPTH, 2, GD_HEADS), F32, math.log(1e-3), math.log(1e-1)))
    return {
        'x_prompt': normal(ks[0], (BATCH, SEQ, D_MODEL)),
        'x_sample': normal(ks[1], (DEC_BATCH, DEC_SEQ, D_MODEL)),
        'mem_prompt': normal(ks[2], (BATCH, N_MEM, D_MODEL)),
        'mem_sample': normal(ks[3], (DEC_BATCH, N_MEM, D_MODEL)),
        'w_in': dense(ks[4], (DEPTH, D_MODEL, IN_COLS), D_MODEL),
        'hgrn_lb': normal(ks[5], (DEPTH + 1, 2, HG_HEADS * HG_DK), 0.1),
        'hgrn_norm_w': 1.0 + normal(ks[6], (DEPTH, HG_DV), 0.02),
        'gdn_conv_w': dense(ks[7], (DEPTH, CONV_K, 1, 2 * GD_HEADS * GD_DK + GD_HEADS * GD_DV), CONV_K),
        'gdn_a_log': jnp.log(jax.random.uniform(ks[8], (DEPTH, 2, GD_HEADS), F32, 1.0, 16.0)),
        'gdn_dt_bias': dt + jnp.log(-jnp.expm1(-dt)),
        'gdn_norm_w': 1.0 + normal(ks[10], (DEPTH, GD_DV), 0.02),
        'w_out': dense(ks[11], (DEPTH, MIX_WIDTH, D_MODEL), MIX_WIDTH, DN_BETA),
        'ln1_g': 1.0 + normal(ks[12], (DEPTH, D_MODEL), 0.02),
        'ln1_b': normal(ks[13], (DEPTH, D_MODEL), 0.02),
        'xa_w_q': dense(ks[14], (DEPTH, D_MODEL, D_MODEL), D_MODEL),
        'xa_w_kv': jnp.concatenate([dense(ks[15], (DEPTH, D_MODEL, D_MODEL), D_MODEL),
                                    dense(ks[16], (DEPTH, D_MODEL, D_MODEL), D_MODEL, DN_BETA)], axis=-1),
        'xa_w_o': dense(ks[17], (DEPTH, D_MODEL, D_MODEL), D_MODEL, DN_BETA),
        'ln2_g': 1.0 + normal(ks[18], (DEPTH, D_MODEL), 0.02),
        'ln2_b': normal(ks[19], (DEPTH, D_MODEL), 0.02),
        'moe_w_group': dense(ks[20], (DEPTH, D_MODEL, N_GROUPS), D_MODEL),
        'moe_b_group': normal(ks[21], (DEPTH, N_GROUPS), 0.01),
        'moe_w_expert': dense(ks[22], (DEPTH, D_MODEL, N_EXPERTS), D_MODEL),
        'moe_b_expert': normal(ks[23], (DEPTH, N_EXPERTS), 0.01),
        'moe_w_gate': dense(ks[24], (DEPTH, N_EXPERTS, D_MODEL, D_EXPERT), D_MODEL),
        'moe_w_up': dense(ks[25], (DEPTH, N_EXPERTS, D_MODEL, D_EXPERT), D_MODEL),
        'moe_w_down': dense(ks[26], (DEPTH, N_EXPERTS, D_EXPERT, D_MODEL), D_EXPERT, DN_BETA),
        'ln3_g': 1.0 + normal(ks[27], (DEPTH, D_MODEL), 0.02),
        'ln3_b': normal(ks[28], (DEPTH, D_MODEL), 0.02),
    }


def reference(x_prompt, x_sample, mem_prompt, mem_sample, w_in, hgrn_lb, hgrn_norm_w, gdn_conv_w,
              gdn_a_log, gdn_dt_bias, gdn_norm_w, w_out, ln1_g, ln1_b, xa_w_q, xa_w_kv, xa_w_o,
              ln2_g, ln2_b, moe_w_group, moe_b_group, moe_w_expert, moe_b_expert, moe_w_gate,
              moe_w_up, moe_w_down, ln3_g, ln3_b):
    y_prompt = _trunk(x_prompt, mem_prompt, w_in, hgrn_lb, hgrn_norm_w, gdn_conv_w, gdn_a_log,
                      gdn_dt_bias, gdn_norm_w, w_out, ln1_g, ln1_b, xa_w_q, xa_w_kv, xa_w_o, ln2_g,
                      ln2_b, moe_w_group, moe_b_group, moe_w_expert, moe_b_expert, moe_w_gate,
                      moe_w_up, moe_w_down, ln3_g, ln3_b)
    y_sample = _trunk(x_sample, mem_sample, w_in, hgrn_lb, hgrn_norm_w, gdn_conv_w, gdn_a_log,
                      gdn_dt_bias, gdn_norm_w, w_out, ln1_g, ln1_b, xa_w_q, xa_w_kv, xa_w_o, ln2_g,
                      ln2_b, moe_w_group, moe_b_group, moe_w_expert, moe_b_expert, moe_w_gate,
                      moe_w_up, moe_w_down, ln3_g, ln3_b)
    return (y_prompt, y_sample)
```

```python
import functools
import math

import numpy as np
import jax
import jax.numpy as jnp
from jax import lax
from jax.experimental import pallas as pl
from jax.experimental.pallas import tpu as pltpu

F32 = jnp.float32
BF16 = jnp.bfloat16

D_MODEL = 2048
N_HEADS = 8
D_HEAD = 128
SEG = N_HEADS * D_HEAD
CONV_K = 5
XA_HEADS = 4
XA_HEAD_DIM = D_MODEL // XA_HEADS
N_GROUPS = 4
EXPERTS_PER_GROUP = 8
N_EXPERTS = N_GROUPS * EXPERTS_PER_GROUP
D_EXPERT = 512
DN_ALPHA = 2.0 ** 0.25
LN_EPS = 1e-5
NORM_EPS = 1e-6

LANES = 128
SUBLANES = 8
VMEM_LIMIT = 56 * 1024 * 1024

HG_CHUNK = 128
GD_CHUNK = 64
MOE_TB = 256


def _cparams(sem):
    return pltpu.CompilerParams(dimension_semantics=sem, vmem_limit_bytes=VMEM_LIMIT)


def _dot(a, b):
    return jnp.dot(a, b, preferred_element_type=F32)


def _dot_nt(a, b):
    return lax.dot_general(a, b, (((1,), (1,)), ((), ())), preferred_element_type=F32)


def _dot_tn(a, b):
    return lax.dot_general(a, b, (((0,), (0,)), ((), ())), preferred_element_type=F32)


def _split2(x):
    hi = x.astype(BF16)
    mid = (x - hi.astype(F32)).astype(BF16)
    return hi, mid


def _sigmoid(x):
    return 1.0 / (1.0 + jnp.exp(-x))


def _layernorm_rows(y, g, b):
    mu = jnp.mean(y, axis=-1, keepdims=True)
    yc = y - mu
    var = jnp.mean(yc * yc, axis=-1, keepdims=True)
    return yc * lax.rsqrt(var + LN_EPS) * g + b


def _mm_kernel(x_ref, w_ref, o_ref):
    o_ref[...] = _dot(x_ref[...], w_ref[...]).astype(o_ref.dtype)


def _matmul(x, w, tm, tn, out_dtype):
    m, k = x.shape
    n = w.shape[1]
    return pl.pallas_call(
        _mm_kernel,
        grid=(m // tm, n // tn),
        in_specs=[pl.BlockSpec((tm, k), lambda i, j: (i, 0)),
                  pl.BlockSpec((k, tn), lambda i, j: (0, j))],
        out_specs=pl.BlockSpec((tm, tn), lambda i, j: (i, j)),
        out_shape=jax.ShapeDtypeStruct((m, n), out_dtype),
        compiler_params=_cparams(("parallel", "arbitrary")),
        name="dense_matmul",
    )(x, w)


def _hgrn_constants(c):
    nl = int(math.log2(c))
    t = np.arange(c)
    u = np.arange(c)
    blocks = []
    blocks.append((u[None, :] <= t[:, None]).astype(np.float32))
    blocks.append((u[None, :] > t[:, None]).astype(np.float32))
    masks, qrows = [], []
    for l in range(nl):
        cc = 1 << l
        m = (t // (2 * cc)) * (2 * cc) + cc
        upper = ((t >> l) & 1) == 1
        a = np.zeros((c, c), np.float32)
        for r in range(c):
            if upper[r]:
                a[r, m[r]:r + 1] = 1.0
            else:
                a[r, r + 1:m[r]] = 1.0
        blocks.append(a)
        same = (t[:, None] >> (l + 1)) == (t[None, :] >> (l + 1))
        masks.append((same & upper[:, None] & (~upper)[None, :]).astype(np.float32))
        qrows.append(np.broadcast_to(upper[:, None], (c, D_HEAD)).astype(np.float32))
    blocks.append(np.ones((16, c), np.float32))
    a_f = np.concatenate(blocks, axis=0)
    m_f = np.stack(masks)
    q_f = np.stack(qrows)
    a_b = np.concatenate([b[::-1, ::-1] for b in blocks], axis=0)
    m_b = m_f[:, ::-1, ::-1]
    q_b = q_f[:, ::-1, :]
    return (np.stack([a_f, a_b]), np.stack([m_f, m_b]).copy(), np.stack([q_f, q_b]).copy(), nl)


def _hgrn_kernel(q_ref, f_ref, v_ref, lb_ref, a_ref, m_ref, qr_ref, o_ref, st_ref, *, c, nck, nl, scale):
    d = pl.program_id(0)
    i = pl.program_id(3)

    @pl.when(i == 0)
    def _():
        st_ref[...] = jnp.zeros_like(st_ref)

    lb = lb_ref[0:1, :]
    one_m_lb = 1.0 - lb

    def eblk(k, g2, rows=c):
        e2 = _dot(a_ref[pl.ds(k * c, rows), :], g2)
        return e2[:, :D_HEAD] + e2[:, D_HEAD:]

    for j in range(nck):
        jj = jnp.where(d == 0, j, nck - 1 - j)
        r0 = pl.multiple_of(jj * c, c)
        fr = f_ref[pl.ds(r0, c), :]
        qs = q_ref[pl.ds(r0, c), :] * scale
        v = v_ref[pl.ds(r0, c), :]
        sig = _sigmoid(fr)
        g = jnp.log(lb + one_m_lb * sig)
        kk = one_m_lb * _sigmoid(-fr)
        g_hi, g_mid = _split2(g)
        g2 = jnp.concatenate([g_hi, g_mid], axis=1)
        gcum = eblk(0, g2)
        gsuf = eblk(1, g2)
        gtot = eblk(2 + nl, g2, rows=16)[0:1, :]
        qd = (qs * jnp.exp(gcum)).astype(BF16)
        kd = (kk * jnp.exp(gsuf)).astype(BF16)
        s = jnp.zeros((c, c), F32)
        for l in range(nl):
            z = jnp.exp(eblk(2 + l, g2))
            w = (jnp.where(qr_ref[l] > 0.5, qs, kk) * z).astype(BF16)
            s = s + m_ref[l] * _dot_nt(w, w)
        dg = jnp.sum(qs * kk, axis=-1, keepdims=True)
        vb = v.astype(BF16)
        st = st_ref[...]
        o = _dot(s.astype(BF16), vb) + _dot_nt(qd, st.astype(BF16)) + dg * v
        st_ref[...] = st * jnp.exp(gtot) + _dot_tn(vb, kd)
        o_ref[pl.ds(r0, c), :] = o


def _hgrn2(proj, lb2, nb, t, tbk):
    c = HG_CHUNK
    n = nb * t
    nblk = t // tbk
    a_np, m_np, q_np, nl = _hgrn_constants(c)
    a_all = jnp.asarray(a_np, BF16)
    masks = jnp.asarray(m_np, F32)
    qrows = jnp.asarray(q_np, F32)
    lb8 = jnp.broadcast_to(lb2.reshape(2, N_HEADS, 1, D_HEAD), (2, N_HEADS, SUBLANES, D_HEAD))

    def row(d, b, i):
        return b * nblk + jnp.where(d == 0, i, nblk - 1 - i)

    kern = functools.partial(_hgrn_kernel, c=c, nck=tbk // c, nl=nl, scale=D_HEAD ** -0.5)
    na = a_np.shape[1]
    return pl.pallas_call(
        kern,
        grid=(2, nb, N_HEADS, nblk),
        in_specs=[
            pl.BlockSpec((tbk, D_HEAD), lambda d, b, h, i: (row(d, b, i), h)),
            pl.BlockSpec((tbk, D_HEAD), lambda d, b, h, i: (row(d, b, i), (1 + d) * N_HEADS + h)),
            pl.BlockSpec((tbk, D_HEAD), lambda d, b, h, i: (row(d, b, i), 3 * N_HEADS + h)),
            pl.BlockSpec((None, None, SUBLANES, D_HEAD), lambda d, b, h, i: (d, h, 0, 0)),
            pl.BlockSpec((None, na, c), lambda d, b, h, i: (d, 0, 0)),
            pl.BlockSpec((None, nl, c, c), lambda d, b, h, i: (d, 0, 0, 0)),
            pl.BlockSpec((None, nl, c, D_HEAD), lambda d, b, h, i: (d, 0, 0, 0)),
        ],
        out_specs=pl.BlockSpec((None, tbk, D_HEAD), lambda d, b, h, i: (d, row(d, b, i), h)),
        out_shape=jax.ShapeDtypeStruct((2, n, SEG), F32),
        scratch_shapes=[pltpu.VMEM((D_HEAD, D_HEAD), F32)],
        compiler_params=_cparams(("parallel", "parallel", "parallel", "arbitrary")),
        name="hgrn2_bidir",
    )(proj, proj, proj, lb8, a_all, masks, qrows)


def _gdn_prep_kernel(prev_ref, cur_ref, nxt_ref, w_ref, o_ref, *, tbk, scale):
    i = pl.program_id(1)
    s = pl.program_id(2)
    nblk = pl.num_programs(1)
    prev = jnp.where(i == 0, 0.0, prev_ref[...])
    nxt = jnp.where(i == nblk - 1, 0.0, nxt_ref[...])
    full = jnp.concatenate([prev, cur_ref[...], nxt], axis=0)
    rows = tbk + 2 * SUBLANES
    acc = jnp.zeros((tbk, SEG), F32)
    for j in range(CONV_K):
        shift = (CONV_K // 2 - j) % rows
        sh = full if shift == 0 else pltpu.roll(full, shift, axis=0)
        acc = acc + sh[SUBLANES:SUBLANES + tbk, :] * w_ref[j:j + 1, :]
    y = acc * _sigmoid(acc)
    fac_all = jnp.where(s == 0, scale, 1.0)
    for h in range(N_HEADS):
        ys = y[:, h * D_HEAD:(h + 1) * D_HEAD]
        ss = jnp.sum(ys * ys, axis=-1, keepdims=True)
        fac = jnp.where(s < 2, lax.rsqrt(ss + NORM_EPS), 1.0) * fac_all
        o_ref[:, h * D_HEAD:(h + 1) * D_HEAD] = ys * fac


def _gdn_prep(proj, conv_w, nb, t, tbk):
    n = nb * t
    nblk = t // tbk
    hb = tbk // SUBLANES
    w8 = jnp.concatenate([conv_w, jnp.zeros((SUBLANES - CONV_K, 3 * SEG), F32)], axis=0)
    last = n // SUBLANES - 1
    kern = functools.partial(_gdn_prep_kernel, tbk=tbk, scale=D_HEAD ** -0.5)
    return pl.pallas_call(
        kern,
        grid=(nb, nblk, 3),
        in_specs=[
            pl.BlockSpec((SUBLANES, SEG), lambda b, i, s: (jnp.maximum((b * nblk + i) * hb - 1, 0), 5 + s)),
            pl.BlockSpec((tbk, SEG), lambda b, i, s: (b * nblk + i, 5 + s)),
            pl.BlockSpec((SUBLANES, SEG), lambda b, i, s: (jnp.minimum((b * nblk + i + 1) * hb, last), 5 + s)),
            pl.BlockSpec((SUBLANES, SEG), lambda b, i, s: (0, s)),
        ],
        out_specs=pl.BlockSpec((None, tbk, SEG), lambda b, i, s: (s, b * nblk + i, 0)),
        out_shape=jax.ShapeDtypeStruct((3, n, SEG), F32),
        compiler_params=_cparams(("parallel", "parallel", "arbitrary")),
        name="gdn_conv_prep",
    )(proj, proj, proj, w8)


def _gdn_constants(c):
    nl = int(math.log2(c))
    t = np.arange(c)
    incl = (t[None, :] <= t[:, None]).astype(np.float32)
    strict = (t[None, :] < t[:, None]).astype(np.float32)
    after = (t[:, None] > t[None, :]).astype(np.float32)
    rhs_f = np.concatenate([np.ones((c, LANES), np.float32), after], axis=1)
    lvl = []
    for l in range(nl):
        upper = ((t >> l) & 1) == 1
        same = (t[:, None] >> (l + 1)) == (t[None, :] >> (l + 1))
        lvl.append((same & upper[:, None] & (~upper)[None, :]).astype(np.float32))
    lvl = np.stack(lvl)
    mk_f = np.concatenate([incl[None], strict[None], lvl], axis=0)
    mk_b = mk_f[:, ::-1, ::-1]
    rhs_b = np.concatenate([np.ones((c, LANES), np.float32), after[::-1, ::-1]], axis=1)
    return np.stack([mk_f, mk_b]).copy(), np.stack([rhs_f, rhs_b]).copy(), nl


def _gdn_kernel(coef_ref, q_ref, k_ref, v_ref, arow_ref, ps_ref, mk_ref, rhs_ref, o_ref, st_ref, *, c, nck, nl):
    d = pl.program_id(0)
    h = pl.program_id(2)
    i = pl.program_id(3)

    @pl.when(i == 0)
    def _():
        st_ref[...] = jnp.zeros_like(st_ref)

    neg_ea = coef_ref[0, d * N_HEADS + h]
    dtb = coef_ref[1, d * N_HEADS + h]
    lane = lax.broadcasted_iota(jnp.int32, (c, LANES), 1)
    beta_lane = 2 * N_HEADS + d * N_HEADS + h
    incl = mk_ref[0]
    strict = mk_ref[1]
    eye = incl - strict
    rhs = rhs_ref[...]

    for j in range(nck):
        jj = jnp.where(d == 0, j, nck - 1 - j)
        r0 = pl.multiple_of(jj * c, c)
        qn = q_ref[pl.ds(r0, c), :]
        kn = k_ref[pl.ds(r0, c), :]
        v = v_ref[pl.ds(r0, c), :]
        a_blk = arow_ref[pl.ds(h, 1), :]
        a_row = jnp.where(d == 0, a_blk[:, j * c:(j + 1) * c], a_blk[:, (nck - 1 - j) * c:(nck - j) * c])
        z = a_row + dtb
        g_row = neg_ea * (jnp.maximum(z, 0.0) + jnp.log(1.0 + jnp.exp(-jnp.abs(z))))
        b_col = jnp.sum(jnp.where(lane == beta_lane, ps_ref[pl.ds(r0, c), :], 0.0), axis=-1, keepdims=True)
        beta = _sigmoid(b_col)
        lg = incl * g_row
        lg_hi, lg_mid = _split2(lg)
        out = _dot(jnp.concatenate([lg_hi, lg_mid], axis=0), rhs)
        out = out[:c] + out[c:]
        gcum = out[:, :LANES]
        dmat = out[:, LANES:]
        gtot = jnp.where(d == 0, gcum[c - 1:c, :], gcum[0:1, :])
        dec = jnp.exp(jnp.minimum(dmat, 0.0)) * incl
        eg = jnp.exp(gcum)
        egs = jnp.exp(gtot - gcum)
        etot = jnp.exp(gtot)
        kb = kn * beta
        kn_b = kn.astype(BF16)
        a_dec = _dot_nt(kb.astype(BF16), kn_b) * dec
        tinv = eye - a_dec * mk_ref[2]
        for l in range(1, nl):
            ao = (a_dec * mk_ref[2 + l]).astype(BF16)
            tb = tinv.astype(BF16)
            tinv = tinv - _dot(_dot(tb, ao).astype(BF16), tb)
        uw = _dot(tinv.astype(BF16), jnp.concatenate([v * beta, kb * eg], axis=1).astype(BF16))
        u = uw[:, :D_HEAD]
        w = uw[:, D_HEAD:]
        qk = _dot_nt(qn.astype(BF16), kn_b) * dec
        st = st_ref[...]
        stb = st.astype(BF16)
        vn = u - _dot(w.astype(BF16), stb)
        vnb = vn.astype(BF16)
        o = _dot((qn * eg).astype(BF16), stb) + _dot(qk.astype(BF16), vnb)
        st_ref[...] = st * etot + _dot_tn((kn * egs).astype(BF16), vnb)
        o_ref[pl.ds(r0, c), :] = o


def _gdn(qkv, ps, ps_t, coef, nb, t, tbk):
    c = GD_CHUNK
    n = nb * t
    nblk = t // tbk
    mk_np, rhs_np, nl = _gdn_constants(c)
    mk = jnp.asarray(mk_np, F32)
    rhs = jnp.asarray(rhs_np, BF16)

    def row(d, b, i):
        return b * nblk + jnp.where(d == 0, i, nblk - 1 - i)

    kern = functools.partial(_gdn_kernel, c=c, nck=tbk // c, nl=nl)
    grid_spec = pltpu.PrefetchScalarGridSpec(
        num_scalar_prefetch=0,
        grid=(2, nb, N_HEADS, nblk),
        in_specs=[
            pl.BlockSpec(memory_space=pltpu.SMEM),
            pl.BlockSpec((None, tbk, D_HEAD), lambda d, b, h, i: (0, row(d, b, i), h)),
            pl.BlockSpec((None, tbk, D_HEAD), lambda d, b, h, i: (1, row(d, b, i), h)),
            pl.BlockSpec((None, tbk, D_HEAD), lambda d, b, h, i: (2, row(d, b, i), h)),
            pl.BlockSpec((SUBLANES, tbk), lambda d, b, h, i: (d, row(d, b, i))),
            pl.BlockSpec((tbk, LANES), lambda d, b, h, i: (row(d, b, i), 0)),
            pl.BlockSpec((None, 2 + nl, c, c), lambda d, b, h, i: (d, 0, 0, 0)),
            pl.BlockSpec((None, c, c + LANES), lambda d, b, h, i: (d, 0, 0)),
        ],
        out_specs=pl.BlockSpec((None, tbk, D_HEAD), lambda d, b, h, i: (d, row(d, b, i), h)),
        scratch_shapes=[pltpu.VMEM((D_HEAD, D_HEAD), F32)],
    )
    return pl.pallas_call(
        kern,
        grid_spec=grid_spec,
        out_shape=jax.ShapeDtypeStruct((2, n, SEG), F32),
        compiler_params=_cparams(("parallel", "parallel", "parallel", "arbitrary")),
        name="gdn_bidir",
    )(coef, qkv, qkv, qkv, ps_t, ps, mk, rhs)


def _mixout_kernel(oh_ref, og_ref, hg_ref, gz_ref, x_ref, w_ref, nw_ref, ln_ref, o_ref, lhs_ref):
    def gated(o2_ref, gate_ref, wrow, col0):
        for h in range(N_HEADS):
            sl = slice(h * D_HEAD, (h + 1) * D_HEAD)
            o = o2_ref[0, :, sl] + o2_ref[1, :, sl]
            ms = jnp.mean(o * o, axis=-1, keepdims=True)
            gt = gate_ref[:, sl]
            y = o * lax.rsqrt(ms + NORM_EPS) * wrow * (gt * _sigmoid(gt))
            lhs_ref[:, col0 + h * D_HEAD:col0 + (h + 1) * D_HEAD] = y.astype(BF16)

    gated(oh_ref, hg_ref, nw_ref[0:1, :], 0)
    gated(og_ref, gz_ref, nw_ref[1:2, :], SEG)
    hmix = _dot(lhs_ref[...], w_ref[...])
    y = DN_ALPHA * x_ref[...] + hmix
    o_ref[...] = _layernorm_rows(y, ln_ref[0:1, :], ln_ref[1:2, :])


def _mixout(oh, og, proj, x, w_out_b, nw, ln, tm):
    n = x.shape[0]
    return pl.pallas_call(
        _mixout_kernel,
        grid=(n // tm,),
        in_specs=[
            pl.BlockSpec((2, tm, SEG), lambda i: (0, i, 0)),
            pl.BlockSpec((2, tm, SEG), lambda i: (0, i, 0)),
            pl.BlockSpec((tm, SEG), lambda i: (i, 4)),
            pl.BlockSpec((tm, SEG), lambda i: (i, 8)),
            pl.BlockSpec((tm, D_MODEL), lambda i: (i, 0)),
            pl.BlockSpec((2 * SEG, D_MODEL), lambda i: (0, 0)),
            pl.BlockSpec((SUBLANES, D_HEAD), lambda i: (0, 0)),
            pl.BlockSpec((SUBLANES, D_MODEL), lambda i: (0, 0)),
        ],
        out_specs=pl.BlockSpec((tm, D_MODEL), lambda i: (i, 0)),
        out_shape=jax.ShapeDtypeStruct((n, D_MODEL), F32),
        scratch_shapes=[pltpu.VMEM((tm, 2 * SEG), BF16)],
        compiler_params=_cparams(("parallel",)),
        name="mixer_out_ln1",
    )(oh, og, proj, proj, x, w_out_b, nw, ln)


def _attn_kernel(x_ref, kv_ref, wq_ref, wo_ref, ln_ref, wr_ref, br_ref, o_ref, r_ref, att_ref):
    x1 = x_ref[...]
    q = _dot(x1.astype(BF16), wq_ref[...])
    sc = XA_HEAD_DIM ** -0.5
    for h in range(XA_HEADS):
        sl = slice(h * XA_HEAD_DIM, (h + 1) * XA_HEAD_DIM)
        kh = kv_ref[:, sl]
        vh = kv_ref[:, D_MODEL + h * XA_HEAD_DIM:D_MODEL + (h + 1) * XA_HEAD_DIM]
        s = _dot_nt(q[:, sl].astype(BF16), kh) * sc
        s = s - jnp.max(s, axis=-1, keepdims=True)
        p = jnp.exp(s)
        p = p / jnp.sum(p, axis=-1, keepdims=True)
        att_ref[:, sl] = _dot(p.astype(BF16), vh).astype(BF16)
    hx = _dot(att_ref[...], wo_ref[...])
    x2 = _layernorm_rows(DN_ALPHA * x1 + hx, ln_ref[0:1, :], ln_ref[1:2, :])
    o_ref[...] = x2

    x_hi, x_mid = _split2(x2)
    lg2 = _dot(x_hi, wr_ref[...])
    lg = lg2[:, :LANES] + lg2[:, LANES:] + _dot(x_mid, wr_ref[:, :LANES]) + br_ref[0:1, :]
    tm = lg.shape[0]
    lane_i = lax.broadcasted_iota(jnp.int32, (tm, LANES), 1)
    lane = lane_i.astype(F32)
    grp_of_lane = (lane_i >> 3).astype(F32)
    neg = jnp.float32(-1e30)
    big = jnp.float32(LANES)
    is_g = (lane_i >= N_EXPERTS) & (lane_i < N_EXPERTS + N_GROUPS)
    gl = jnp.where(is_g, lg, neg)
    gmax = jnp.max(gl, axis=-1, keepdims=True)
    gsel = jnp.min(jnp.where(gl == gmax, lane, big), axis=-1, keepdims=True) - N_EXPERTS
    p_group = 1.0 / jnp.sum(jnp.where(is_g, jnp.exp(gl - gmax), 0.0), axis=-1, keepdims=True)
    in_grp = (lane_i < N_EXPERTS) & (grp_of_lane == gsel)
    el = jnp.where(in_grp, lg, neg)
    m1 = jnp.max(el, axis=-1, keepdims=True)
    i1 = jnp.min(jnp.where(el == m1, lane, big), axis=-1, keepdims=True)
    el2 = jnp.where(lane == i1, neg, el)
    m2 = jnp.max(el2, axis=-1, keepdims=True)
    i2 = jnp.min(jnp.where(el2 == m2, lane, big), axis=-1, keepdims=True)
    e2 = jnp.exp(m2 - m1)
    g1 = p_group / (1.0 + e2)
    g2 = p_group * e2 / (1.0 + e2)
    r = jnp.where(lane_i == 0, i1, 0.0)
    r = jnp.where(lane_i == 1, i2, r)
    r = jnp.where(lane_i == 2, g1, r)
    r = jnp.where(lane_i == 3, g2, r)
    r_ref[...] = r


def _attn(x1, kv, wq_b, wo_b, ln, wr, br, nb, t, tm):
    n = nb * t
    nblk = t // tm
    n_mem = kv.shape[0] // nb
    return pl.pallas_call(
        _attn_kernel,
        grid=(nb, nblk),
        in_specs=[
            pl.BlockSpec((tm, D_MODEL), lambda b, i: (b * nblk + i, 0)),
            pl.BlockSpec((n_mem, 2 * D_MODEL), lambda b, i: (b, 0)),
            pl.BlockSpec((D_MODEL, D_MODEL), lambda b, i: (0, 0)),
            pl.BlockSpec((D_MODEL, D_MODEL), lambda b, i: (0, 0)),
            pl.BlockSpec((SUBLANES, D_MODEL), lambda b, i: (0, 0)),
            pl.BlockSpec((D_MODEL, 2 * LANES), lambda b, i: (0, 0)),
            pl.BlockSpec((SUBLANES, LANES), lambda b, i: (0, 0)),
        ],
        out_specs=[pl.BlockSpec((tm, D_MODEL), lambda b, i: (b * nblk + i, 0)),
                   pl.BlockSpec((tm, LANES), lambda b, i: (b * nblk + i, 0))],
        out_shape=[jax.ShapeDtypeStruct((n, D_MODEL), F32), jax.ShapeDtypeStruct((n, LANES), F32)],
        scratch_shapes=[pltpu.VMEM((tm, D_MODEL), BF16)],
        compiler_params=_cparams(("parallel", "arbitrary")),
        name="xattn_ln2_router",
    )(x1, kv, wq_b, wo_b, ln, wr, br)


def _moe_kernel(be_ref, tok_ref, x_hbm, wg_ref, wu_ref, wd_ref, o_ref, xbuf, sem, *, tb):
    del be_ref

    def row_copy(r):
        return pltpu.make_async_copy(x_hbm.at[pl.ds(tok_ref[0, r], 1)], xbuf.at[pl.ds(r, 1)], sem)

    def start(r, carry):
        row_copy(r).start()
        return carry

    def wait(r, carry):
        row_copy(r).wait()
        return carry

    lax.fori_loop(0, tb, start, 0)
    lax.fori_loop(0, tb, wait, 0)
    xb = xbuf[...].astype(BF16)
    h1 = _dot(xb, wg_ref[...])
    h2 = _dot(xb, wu_ref[...])
    hdn = (h1 * _sigmoid(h1) * h2).astype(BF16)
    o_ref[...] = _dot(hdn, wd_ref[...])


def _moe(x2, block_e, slot_tok, wg_b, wu_b, wd_b):
    tb = MOE_TB
    cap = slot_tok.shape[0]
    n_blocks = cap // tb
    grid_spec = pltpu.PrefetchScalarGridSpec(
        num_scalar_prefetch=1,
        grid=(n_blocks,),
        in_specs=[
            pl.BlockSpec((None, 1, tb), lambda b, be: (b, 0, 0), memory_space=pltpu.SMEM),
            pl.BlockSpec(memory_space=pl.ANY),
            pl.BlockSpec((None, D_MODEL, D_EXPERT), lambda b, be: (be[b], 0, 0)),
            pl.BlockSpec((None, D_MODEL, D_EXPERT), lambda b, be: (be[b], 0, 0)),
            pl.BlockSpec((None, D_EXPERT, D_MODEL), lambda b, be: (be[b], 0, 0)),
        ],
        out_specs=pl.BlockSpec((tb, D_MODEL), lambda b, be: (b, 0)),
        scratch_shapes=[pltpu.VMEM((tb, D_MODEL), F32), pltpu.SemaphoreType.DMA(())],
    )
    return pl.pallas_call(
        functools.partial(_moe_kernel, tb=tb),
        grid_spec=grid_spec,
        out_shape=jax.ShapeDtypeStruct((cap, D_MODEL), F32),
        compiler_params=_cparams(("arbitrary",)),
        name="moe_grouped_mlp",
    )(block_e, slot_tok.reshape(n_blocks, 1, tb), x2, wg_b, wu_b, wd_b)


def _combine_kernel(pos_ref, y_hbm, x_ref, r_ref, ln_ref, o_ref, ybuf, sem, *, tm):
    def row_copy(r):
        return pltpu.make_async_copy(y_hbm.at[pl.ds(pos_ref[0, r], 1)], ybuf.at[pl.ds(r, 1)], sem)

    def start(r, carry):
        row_copy(r).start()
        return carry

    def wait(r, carry):
        row_copy(r).wait()
        return carry

    lax.fori_loop(0, 2 * tm, start, 0)
    lax.fori_loop(0, 2 * tm, wait, 0)
    r = r_ref[...]
    y = r[:, 2:3] * ybuf[0:tm, :] + r[:, 3:4] * ybuf[tm:2 * tm, :]
    o_ref[...] = _layernorm_rows(DN_ALPHA * x_ref[...] + y, ln_ref[0:1, :], ln_ref[1:2, :])


def _combine(y_slots, pos, x2, r, ln, tm):
    n = x2.shape[0]
    grid_spec = pltpu.PrefetchScalarGridSpec(
        num_scalar_prefetch=0,
        grid=(n // tm,),
        in_specs=[
            pl.BlockSpec((None, 1, 2 * tm), lambda i: (i, 0, 0), memory_space=pltpu.SMEM),
            pl.BlockSpec(memory_space=pl.ANY),
            pl.BlockSpec((tm, D_MODEL), lambda i: (i, 0)),
            pl.BlockSpec((tm, LANES), lambda i: (i, 0)),
            pl.BlockSpec((SUBLANES, D_MODEL), lambda i: (0, 0)),
        ],
        out_specs=pl.BlockSpec((tm, D_MODEL), lambda i: (i, 0)),
        scratch_shapes=[pltpu.VMEM((2 * tm, D_MODEL), F32), pltpu.SemaphoreType.DMA(())],
    )
    return pl.pallas_call(
        functools.partial(_combine_kernel, tm=tm),
        grid_spec=grid_spec,
        out_shape=jax.ShapeDtypeStruct((n, D_MODEL), F32),
        compiler_params=_cparams(("arbitrary",)),
        name="moe_combine_ln3",
    )(pos, y_slots, x2, r, ln)


def _route_slots(e1, e2, tb, tm):
    n = e1.shape[0]
    flat_e = jnp.stack([e1, e2], axis=1).reshape(-1)
    n_assign = 2 * n
    order = jnp.argsort(flat_e, stable=True)
    sorted_e = flat_e[order]
    counts = jnp.zeros((N_EXPERTS,), jnp.int32).at[flat_e].add(1)
    padded = (counts + tb - 1) // tb * tb
    pad_end = jnp.cumsum(padded)
    pad_start = pad_end - padded
    start = jnp.cumsum(counts) - counts
    dest = pad_start[sorted_e] + jnp.arange(n_assign, dtype=jnp.int32) - start[sorted_e]
    n_blocks = -(-n_assign // tb) + N_EXPERTS
    cap = n_blocks * tb
    slot_tok = jnp.zeros((cap,), jnp.int32).at[dest].set((order // 2).astype(jnp.int32))
    block_e = jnp.minimum(jnp.searchsorted(pad_end, jnp.arange(n_blocks, dtype=jnp.int32) * tb, side='right'),
                          N_EXPERTS - 1).astype(jnp.int32)
    slot_of = jnp.zeros((n_assign,), jnp.int32).at[order].set(dest.astype(jnp.int32))
    pos = slot_of.reshape(n // tm, tm, 2).transpose(0, 2, 1).reshape(n // tm, 1, 2 * tm)
    return block_e, slot_tok, pos


def _pad_rows(a, rows=SUBLANES):
    return jnp.concatenate([a, jnp.zeros((rows - a.shape[0],) + a.shape[1:], a.dtype)], axis=0)


def _layer(x, mem, w_in, hgrn_lb, hgrn_norm_w, gdn_conv_w, gdn_a_log, gdn_dt_bias, gdn_norm_w, w_out,
           ln1_g, ln1_b, xa_w_q, xa_w_kv, xa_w_o, ln2_g, ln2_b, moe_w_group, moe_b_group,
           moe_w_expert, moe_b_expert, moe_w_gate, moe_w_up, moe_w_down, ln3_g, ln3_b,
           *, tm_mm=1024, tbk=512, tm_row=256):
    nb, t, _ = x.shape
    n = nb * t
    xf = x.reshape(n, D_MODEL)
    layer = 0

    w_in_b = w_in[layer].astype(BF16)
    n_main = 9 * SEG
    w_small = jnp.concatenate([w_in_b[:, n_main:], jnp.zeros((D_MODEL, LANES - 4 * N_HEADS), BF16)], axis=1)
    xb = xf.astype(BF16)
    proj = _matmul(xb, w_in_b[:, :n_main], tm_mm, SEG, F32)
    ps = _matmul(xb, w_small, tm_mm, LANES, F32)

    lb2 = jnp.cumsum(jax.nn.softmax(hgrn_lb.astype(F32), axis=0), axis=0)[layer]
    oh = _hgrn2(proj, lb2, nb, t, tbk)

    qkv = _gdn_prep(proj, gdn_conv_w[layer, :, 0, :].astype(F32), nb, t, tbk)
    coef = jnp.stack([-jnp.exp(gdn_a_log[layer].astype(F32)).reshape(-1),
                      gdn_dt_bias[layer].astype(F32).reshape(-1)], axis=0)
    og = _gdn(qkv, ps, ps.T, coef, nb, t, tbk)

    nw = _pad_rows(jnp.stack([hgrn_norm_w[layer], gdn_norm_w[layer]], axis=0).astype(F32))
    ln1 = _pad_rows(jnp.stack([ln1_g[layer], ln1_b[layer]], axis=0).astype(F32))
    x1 = _mixout(oh, og, proj, xf, w_out[layer].astype(BF16), nw, ln1, tm_row)

    n_mem = mem.shape[1]
    kv = _matmul(mem.reshape(nb * n_mem, D_MODEL).astype(BF16), xa_w_kv[layer].astype(BF16),
                 n_mem, 1024, BF16)
    ln2 = _pad_rows(jnp.stack([ln2_g[layer], ln2_b[layer]], axis=0).astype(F32))
    w_r = jnp.concatenate([moe_w_expert[layer], moe_w_group[layer],
                           jnp.zeros((D_MODEL, LANES - N_EXPERTS - N_GROUPS), F32)], axis=1).astype(F32)
    w_r_hi = w_r.astype(BF16)
    w_r_mid = (w_r - w_r_hi.astype(F32)).astype(BF16)
    wr = jnp.concatenate([w_r_hi, w_r_mid], axis=1)
    b_r = jnp.concatenate([moe_b_expert[layer], moe_b_group[layer],
                           jnp.zeros((LANES - N_EXPERTS - N_GROUPS,), F32)]).astype(F32)
    br = jnp.broadcast_to(b_r[None, :], (SUBLANES, LANES))
    x2, r = _attn(x1, kv, xa_w_q[layer].astype(BF16), xa_w_o[layer].astype(BF16), ln2, wr, br, nb, t, tm_row)

    e1 = r[:, 0].astype(jnp.int32)
    e2 = r[:, 1].astype(jnp.int32)
    block_e, slot_tok, pos = _route_slots(e1, e2, MOE_TB, tm_row)
    y_slots = _moe(x2, block_e, slot_tok, moe_w_gate[layer].astype(BF16),
                   moe_w_up[layer].astype(BF16), moe_w_down[layer].astype(BF16))

    ln3 = _pad_rows(jnp.stack([ln3_g[layer], ln3_b[layer]], axis=0).astype(F32))
    y = _combine(y_slots, pos, x2, r, ln3, tm_row)
    return y.reshape(nb, t, D_MODEL)


def kernel(x_prompt, x_sample, mem_prompt, mem_sample, w_in, hgrn_lb, hgrn_norm_w, gdn_conv_w, gdn_a_log, gdn_dt_bias, gdn_norm_w, w_out, ln1_g, ln1_b, xa_w_q, xa_w_kv, xa_w_o, ln2_g, ln2_b, moe_w_group, moe_b_group, moe_w_expert, moe_b_expert, moe_w_gate, moe_w_up, moe_w_down, ln3_g, ln3_b):
    assert x_prompt.shape[1] == x_sample.shape[1]
    bp = x_prompt.shape[0]
    x = jnp.concatenate([x_prompt, x_sample], axis=0)
    mem = jnp.concatenate([mem_prompt, mem_sample], axis=0)
    y = _layer(x, mem, w_in, hgrn_lb, hgrn_norm_w, gdn_conv_w, gdn_a_log, gdn_dt_bias, gdn_norm_w, w_out,
               ln1_g, ln1_b, xa_w_q, xa_w_kv, xa_w_o, ln2_g, ln2_b, moe_w_group, moe_b_group,
               moe_w_expert, moe_b_expert, moe_w_gate, moe_w_up, moe_w_down, ln3_g, ln3_b)
    return (y[:bp], y[bp:])
```

```python
import functools
import math

import numpy as np
import jax
import jax.numpy as jnp
from jax import lax
from jax.experimental import pallas as pl
from jax.experimental.pallas import tpu as pltpu

F32 = jnp.float32
BF16 = jnp.bfloat16

D_MODEL = 2048
N_HEADS = 8
D_HEAD = 128
SEG = N_HEADS * D_HEAD
CONV_K = 5
XA_HEADS = 4
XA_HEAD_DIM = D_MODEL // XA_HEADS
N_GROUPS = 4
EXPERTS_PER_GROUP = 8
N_EXPERTS = N_GROUPS * EXPERTS_PER_GROUP
D_EXPERT = 512
DN_ALPHA = 2.0 ** 0.25
LN_EPS = 1e-5
NORM_EPS = 1e-6

LANES = 128
SUBLANES = 8
VMEM_LIMIT = 56 * 1024 * 1024

HG_CHUNK = 128
HG_TBK = 256
HG_HPG = 4
HG_SMALL_LEVELS = 3
GD_CHUNK = 128
GD_TBK = 256
GD_HPG = 4
MOE_TB = 256


def _cparams(sem):
    return pltpu.CompilerParams(dimension_semantics=sem, vmem_limit_bytes=VMEM_LIMIT)


def _dot(a, b):
    return jnp.dot(a, b, preferred_element_type=F32)


def _dot_nt(a, b):
    return lax.dot_general(a, b, (((1,), (1,)), ((), ())), preferred_element_type=F32)


def _dot_tn(a, b):
    return lax.dot_general(a, b, (((0,), (0,)), ((), ())), preferred_element_type=F32)


def _split2(x):
    hi = x.astype(BF16)
    mid = (x - hi.astype(F32)).astype(BF16)
    return hi, mid


def _sigmoid(x):
    return 1.0 / (1.0 + jnp.exp(-x))


def _layernorm_rows(y, g, b):
    mu = jnp.mean(y, axis=-1, keepdims=True)
    yc = y - mu
    var = jnp.mean(yc * yc, axis=-1, keepdims=True)
    return yc * lax.rsqrt(var + LN_EPS) * g + b


def _mm_kernel(x_ref, w_ref, o_ref):
    o_ref[...] = _dot(x_ref[...], w_ref[...]).astype(o_ref.dtype)


def _matmul(x, w, tm, tn, out_dtype):
    m, k = x.shape
    n = w.shape[1]
    return pl.pallas_call(
        _mm_kernel,
        grid=(m // tm, n // tn),
        in_specs=[pl.BlockSpec((tm, k), lambda i, j: (i, 0)),
                  pl.BlockSpec((k, tn), lambda i, j: (0, j))],
        out_specs=pl.BlockSpec((tm, tn), lambda i, j: (i, j)),
        out_shape=jax.ShapeDtypeStruct((m, n), out_dtype),
        compiler_params=_cparams(("parallel", "arbitrary")),
        name="dense_matmul",
    )(x, w)


def _hgrn_constants(c):
    nl = int(math.log2(c))
    t = np.arange(c)
    u = np.arange(c)
    blocks = []
    blocks.append((u[None, :] <= t[:, None]).astype(np.float32))
    masks, qrows = [], []
    for l in range(nl):
        cc = 1 << l
        m = (t // (2 * cc)) * (2 * cc) + cc
        upper = ((t >> l) & 1) == 1
        a = np.zeros((c, c), np.float32)
        for r in range(c):
            if upper[r]:
                a[r, m[r]:r + 1] = 1.0
            else:
                a[r, r + 1:m[r]] = 1.0
        if l < HG_SMALL_LEVELS:
            blocks.append(a)
        same = (t[:, None] >> (l + 1)) == (t[None, :] >> (l + 1))
        masks.append((same & upper[:, None] & (~upper)[None, :]).astype(np.float32))
        qrows.append(np.broadcast_to(upper[:, None], (c, D_HEAD)).astype(np.float32))
    blocks.append(np.ones((16, c), np.float32))
    a_f = np.concatenate(blocks, axis=0)
    m_f = np.stack(masks)
    q_f = np.stack(qrows)
    a_b = np.concatenate([b[::-1, ::-1] for b in blocks], axis=0)
    m_b = m_f[:, ::-1, ::-1]
    q_b = q_f[:, ::-1, :]
    return (np.stack([a_f, a_b]), np.stack([m_f, m_b]).copy(), np.stack([q_f, q_b]).copy(), nl)


def _hgrn_kernel(q_ref, f_ref, v_ref, lb_ref, a_ref, m_ref, qr_ref, o_ref, st_ref, *, c, nck, nl, scale, hpg):
    d = pl.program_id(0)
    i = pl.program_id(3)

    @pl.when(i == 0)
    def _():
        st_ref[...] = jnp.zeros_like(st_ref)

    probs = [(j, g) for j in range(nck) for g in range(hpg)]

    def rows(j):
        jj = jnp.where(d == 0, j, nck - 1 - j)
        return pl.ds(pl.multiple_of(jj * c, c), c)

    def cols(g):
        return slice(g * D_HEAD, (g + 1) * D_HEAD)

    def eblk(k, g2, rows=c):
        e2 = _dot(a_ref[pl.ds(k * c, rows), :], g2)
        return e2[:, :D_HEAD] + e2[:, D_HEAD:]

    def boundary(gcum, l):
        cc = 1 << l
        pieces = []
        for p in range(c // (2 * cc)):
            m = p * 2 * cc + cc
            rowv = jnp.where(d == 0, gcum[m - 1:m, :], gcum[m:m + 1, :])
            pieces.append(jnp.broadcast_to(rowv, (2 * cc, D_HEAD)))
        return pieces[0] if len(pieces) == 1 else jnp.concatenate(pieces, axis=0)

    qs, kk, vb, vv, g2, gcum, gtot = {}, {}, {}, {}, {}, {}, {}
    for (j, g) in probs:
        lb = lb_ref[g, 0:1, :]
        one_m_lb = 1.0 - lb
        fr = f_ref[rows(j), cols(g)]
        qs[j, g] = q_ref[rows(j), cols(g)] * scale
        v = v_ref[rows(j), cols(g)]
        vv[j, g] = v
        vb[j, g] = v.astype(BF16)
        gl = jnp.log(lb + one_m_lb * _sigmoid(fr))
        kk[j, g] = one_m_lb * _sigmoid(-fr)
        g_hi, g_mid = _split2(gl)
        g2[j, g] = jnp.concatenate([g_hi, g_mid], axis=1)
    for p in probs:
        gcum[p] = eblk(0, g2[p])
        gtot[p] = eblk(1 + HG_SMALL_LEVELS, g2[p], rows=16)[0:1, :]

    s = {}
    for l in range(nl):
        if l < HG_SMALL_LEVELS:
            e = {p: eblk(1 + l, g2[p]) for p in probs}
        else:
            e = {p: -jnp.abs(gcum[p] - boundary(gcum[p], l)) for p in probs}
        w = {p: (jnp.where(qr_ref[l] > 0.5, qs[p], kk[p]) * jnp.exp(e[p])).astype(BF16) for p in probs}
        sd = {p: _dot_nt(w[p], w[p]) for p in probs}
        for p in probs:
            s[p] = m_ref[l] * sd[p] if l == 0 else s[p] + m_ref[l] * sd[p]

    o_in, qd, upd, dec = {}, {}, {}, {}
    for p in probs:
        dg = jnp.sum(qs[p] * kk[p], axis=-1, keepdims=True)
        o_in[p] = _dot(s[p].astype(BF16), vb[p]) + dg * vv[p]
        qd[p] = (qs[p] * jnp.exp(gcum[p])).astype(BF16)
        kd = (kk[p] * jnp.exp(gtot[p] - gcum[p])).astype(BF16)
        upd[p] = _dot_tn(vb[p], kd)
        dec[p] = jnp.exp(gtot[p])

    st = [st_ref[g] for g in range(hpg)]
    for j in range(nck):
        for g in range(hpg):
            p = (j, g)
            o_ref[rows(j), cols(g)] = o_in[p] + _dot_nt(qd[p], st[g].astype(BF16))
            st[g] = st[g] * dec[p] + upd[p]
    for g in range(hpg):
        st_ref[g] = st[g]


def _hgrn2(proj, lb2, nb, t, tbk, hpg):
    c = HG_CHUNK
    n = nb * t
    nblk = t // tbk
    a_np, m_np, q_np, nl = _hgrn_constants(c)
    a_all = jnp.asarray(a_np, BF16)
    masks = jnp.asarray(m_np, F32)
    qrows = jnp.asarray(q_np, F32)
    lb8 = jnp.broadcast_to(lb2.reshape(2, N_HEADS, 1, D_HEAD), (2, N_HEADS, SUBLANES, D_HEAD))
    ngrp = N_HEADS // hpg
    wb = hpg * D_HEAD

    def row(d, b, i):
        return b * nblk + jnp.where(d == 0, i, nblk - 1 - i)

    kern = functools.partial(_hgrn_kernel, c=c, nck=tbk // c, nl=nl, scale=D_HEAD ** -0.5, hpg=hpg)
    na = a_np.shape[1]
    return pl.pallas_call(
        kern,
        grid=(2, nb, ngrp, nblk),
        in_specs=[
            pl.BlockSpec((tbk, wb), lambda d, b, h, i: (row(d, b, i), h)),
            pl.BlockSpec((tbk, wb), lambda d, b, h, i: (row(d, b, i), (1 + d) * ngrp + h)),
            pl.BlockSpec((tbk, wb), lambda d, b, h, i: (row(d, b, i), 3 * ngrp + h)),
            pl.BlockSpec((None, hpg, SUBLANES, D_HEAD), lambda d, b, h, i: (d, h, 0, 0)),
            pl.BlockSpec((None, na, c), lambda d, b, h, i: (d, 0, 0)),
            pl.BlockSpec((None, nl, c, c), lambda d, b, h, i: (d, 0, 0, 0)),
            pl.BlockSpec((None, nl, c, D_HEAD), lambda d, b, h, i: (d, 0, 0, 0)),
        ],
        out_specs=pl.BlockSpec((None, tbk, wb), lambda d, b, h, i: (d, row(d, b, i), h)),
        out_shape=jax.ShapeDtypeStruct((2, n, SEG), F32),
        scratch_shapes=[pltpu.VMEM((hpg, D_HEAD, D_HEAD), F32)],
        compiler_params=_cparams(("parallel", "parallel", "parallel", "arbitrary")),
        name="hgrn2_bidir",
    )(proj, proj, proj, lb8, a_all, masks, qrows)


def _gdn_prep_kernel(prev_ref, cur_ref, nxt_ref, w_ref, o_ref, *, tbk, scale):
    i = pl.program_id(1)
    s = pl.program_id(2)
    nblk = pl.num_programs(1)
    prev = jnp.where(i == 0, 0.0, prev_ref[...])
    nxt = jnp.where(i == nblk - 1, 0.0, nxt_ref[...])
    full = jnp.concatenate([prev, cur_ref[...], nxt], axis=0)
    rows = tbk + 2 * SUBLANES
    acc = jnp.zeros((tbk, SEG), F32)
    for j in range(CONV_K):
        shift = (CONV_K // 2 - j) % rows
        sh = full if shift == 0 else pltpu.roll(full, shift, axis=0)
        acc = acc + sh[SUBLANES:SUBLANES + tbk, :] * w_ref[j:j + 1, :]
    y = acc * _sigmoid(acc)
    fac_all = jnp.where(s == 0, scale, 1.0)
    for h in range(N_HEADS):
        ys = y[:, h * D_HEAD:(h + 1) * D_HEAD]
        ss = jnp.sum(ys * ys, axis=-1, keepdims=True)
        fac = jnp.where(s < 2, lax.rsqrt(ss + NORM_EPS), 1.0) * fac_all
        o_ref[:, h * D_HEAD:(h + 1) * D_HEAD] = ys * fac


def _gdn_prep(proj, conv_w, nb, t, tbk):
    n = nb * t
    nblk = t // tbk
    hb = tbk // SUBLANES
    w8 = jnp.concatenate([conv_w, jnp.zeros((SUBLANES - CONV_K, 3 * SEG), F32)], axis=0)
    last = n // SUBLANES - 1
    kern = functools.partial(_gdn_prep_kernel, tbk=tbk, scale=D_HEAD ** -0.5)
    return pl.pallas_call(
        kern,
        grid=(nb, nblk, 3),
        in_specs=[
            pl.BlockSpec((SUBLANES, SEG), lambda b, i, s: (jnp.maximum((b * nblk + i) * hb - 1, 0), 5 + s)),
            pl.BlockSpec((tbk, SEG), lambda b, i, s: (b * nblk + i, 5 + s)),
            pl.BlockSpec((SUBLANES, SEG), lambda b, i, s: (jnp.minimum((b * nblk + i + 1) * hb, last), 5 + s)),
            pl.BlockSpec((SUBLANES, SEG), lambda b, i, s: (0, s)),
        ],
        out_specs=pl.BlockSpec((None, tbk, SEG), lambda b, i, s: (s, b * nblk + i, 0)),
        out_shape=jax.ShapeDtypeStruct((3, n, SEG), F32),
        compiler_params=_cparams(("parallel", "parallel", "arbitrary")),
        name="gdn_conv_prep",
    )(proj, proj, proj, w8)


def _gdn_constants(c):
    nl = int(math.log2(c))
    t = np.arange(c)
    incl = (t[None, :] <= t[:, None]).astype(np.float32)
    strict = (t[None, :] < t[:, None]).astype(np.float32)
    after = (t[:, None] > t[None, :]).astype(np.float32)
    rhs_f = np.concatenate([np.ones((c, LANES), np.float32), after], axis=1)
    lvl = []
    for l in range(nl):
        upper = ((t >> l) & 1) == 1
        same = (t[:, None] >> (l + 1)) == (t[None, :] >> (l + 1))
        lvl.append((same & upper[:, None] & (~upper)[None, :]).astype(np.float32))
    lvl = np.stack(lvl)
    mk_f = np.concatenate([incl[None], strict[None], lvl], axis=0)
    mk_b = mk_f[:, ::-1, ::-1]
    rhs_b = np.concatenate([np.ones((c, LANES), np.float32), after[::-1, ::-1]], axis=1)
    return np.stack([mk_f, mk_b]).copy(), np.stack([rhs_f, rhs_b]).copy(), nl


def _gdn_kernel(coef_ref, q_ref, k_ref, v_ref, arow_ref, ps_ref, mk_ref, rhs_ref, o_ref, st_ref, *, c, nck, nl, hpg):
    d = pl.program_id(0)
    hg = pl.program_id(2)
    i = pl.program_id(3)

    @pl.when(i == 0)
    def _():
        st_ref[...] = jnp.zeros_like(st_ref)

    lane = lax.broadcasted_iota(jnp.int32, (c, LANES), 1)
    incl = mk_ref[0]
    strict = mk_ref[1]
    eye = incl - strict
    rhs = rhs_ref[...]
    probs = [(j, g) for j in range(nck) for g in range(hpg)]

    def rows(j):
        jj = jnp.where(d == 0, j, nck - 1 - j)
        return pl.ds(pl.multiple_of(jj * c, c), c)

    def cols(g):
        return slice(g * D_HEAD, (g + 1) * D_HEAD)

    qn, kn, vv, beta, gcum, dec, etot = {}, {}, {}, {}, {}, {}, {}
    for (j, g) in probs:
        h = hg * hpg + g
        neg_ea = coef_ref[0, d * N_HEADS + h]
        dtb = coef_ref[1, d * N_HEADS + h]
        a_blk = arow_ref[pl.ds(h, 1), :]
        a_row = jnp.where(d == 0, a_blk[:, j * c:(j + 1) * c], a_blk[:, (nck - 1 - j) * c:(nck - j) * c])
        z = a_row + dtb
        g_row = neg_ea * (jnp.maximum(z, 0.0) + jnp.log(1.0 + jnp.exp(-jnp.abs(z))))
        beta_lane = 2 * N_HEADS + d * N_HEADS + h
        b_col = jnp.sum(jnp.where(lane == beta_lane, ps_ref[rows(j), :], 0.0), axis=-1, keepdims=True)
        beta[j, g] = _sigmoid(b_col)
        lg = incl * g_row
        lg_hi, lg_mid = _split2(lg)
        out = _dot(jnp.concatenate([lg_hi, lg_mid], axis=0), rhs)
        out = out[:c] + out[c:]
        gcum[j, g] = out[:, :LANES]
        dmat = out[:, LANES:]
        dec[j, g] = jnp.exp(jnp.minimum(dmat, 0.0)) * incl
        qn[j, g] = q_ref[rows(j), cols(g)]
        kn[j, g] = k_ref[rows(j), cols(g)]
        vv[j, g] = v_ref[rows(j), cols(g)]

    kb, knb, a_dec, tinv = {}, {}, {}, {}
    for p in probs:
        kb[p] = kn[p] * beta[p]
        knb[p] = kn[p].astype(BF16)
        a_dec[p] = _dot_nt(kb[p].astype(BF16), knb[p]) * dec[p]
        tinv[p] = eye - a_dec[p] * mk_ref[2]
    for l in range(1, nl):
        ml = mk_ref[2 + l]
        tb = {p: tinv[p].astype(BF16) for p in probs}
        ta = {p: _dot(tb[p], (a_dec[p] * ml).astype(BF16)).astype(BF16) for p in probs}
        for p in probs:
            tinv[p] = tinv[p] - _dot(ta[p], tb[p])

    u, w, qk, qe, ke = {}, {}, {}, {}, {}
    for p in probs:
        gtot = jnp.where(d == 0, gcum[p][c - 1:c, :], gcum[p][0:1, :])
        eg = jnp.exp(gcum[p])
        etot[p] = jnp.exp(gtot)
        uw = _dot(tinv[p].astype(BF16), jnp.concatenate([vv[p] * beta[p], kb[p] * eg], axis=1).astype(BF16))
        u[p] = uw[:, :D_HEAD]
        w[p] = uw[:, D_HEAD:].astype(BF16)
        qk[p] = (_dot_nt(qn[p].astype(BF16), knb[p]) * dec[p]).astype(BF16)
        qe[p] = (qn[p] * eg).astype(BF16)
        ke[p] = (kn[p] * jnp.exp(gtot - gcum[p])).astype(BF16)

    st = [st_ref[g] for g in range(hpg)]
    for j in range(nck):
        stb = [st[g].astype(BF16) for g in range(hpg)]
        ws = [_dot(w[j, g], stb[g]) for g in range(hpg)]
        qs = [_dot(qe[j, g], stb[g]) for g in range(hpg)]
        vnb = [(u[j, g] - ws[g]).astype(BF16) for g in range(hpg)]
        for g in range(hpg):
            o_ref[rows(j), cols(g)] = qs[g] + _dot(qk[j, g], vnb[g])
        for g in range(hpg):
            st[g] = st[g] * etot[j, g] + _dot_tn(ke[j, g], vnb[g])
    for g in range(hpg):
        st_ref[g] = st[g]


def _gdn(qkv, ps, ps_t, coef, nb, t, tbk, hpg):
    c = GD_CHUNK
    n = nb * t
    nblk = t // tbk
    mk_np, rhs_np, nl = _gdn_constants(c)
    mk = jnp.asarray(mk_np, F32)
    rhs = jnp.asarray(rhs_np, BF16)
    wb = hpg * D_HEAD

    def row(d, b, i):
        return b * nblk + jnp.where(d == 0, i, nblk - 1 - i)

    kern = functools.partial(_gdn_kernel, c=c, nck=tbk // c, nl=nl, hpg=hpg)
    grid_spec = pltpu.PrefetchScalarGridSpec(
        num_scalar_prefetch=0,
        grid=(2, nb, N_HEADS // hpg, nblk),
        in_specs=[
            pl.BlockSpec(memory_space=pltpu.SMEM),
            pl.BlockSpec((None, tbk, wb), lambda d, b, h, i: (0, row(d, b, i), h)),
            pl.BlockSpec((None, tbk, wb), lambda d, b, h, i: (1, row(d, b, i), h)),
            pl.BlockSpec((None, tbk, wb), lambda d, b, h, i: (2, row(d, b, i), h)),
            pl.BlockSpec((SUBLANES, tbk), lambda d, b, h, i: (d, row(d, b, i))),
            pl.BlockSpec((tbk, LANES), lambda d, b, h, i: (row(d, b, i), 0)),
            pl.BlockSpec((None, 2 + nl, c, c), lambda d, b, h, i: (d, 0, 0, 0)),
            pl.BlockSpec((None, c, c + LANES), lambda d, b, h, i: (d, 0, 0)),
        ],
        out_specs=pl.BlockSpec((None, tbk, wb), lambda d, b, h, i: (d, row(d, b, i), h)),
        scratch_shapes=[pltpu.VMEM((hpg, D_HEAD, D_HEAD), F32)],
    )
    return pl.pallas_call(
        kern,
        grid_spec=grid_spec,
        out_shape=jax.ShapeDtypeStruct((2, n, SEG), F32),
        compiler_params=_cparams(("parallel", "parallel", "parallel", "arbitrary")),
        name="gdn_bidir",
    )(coef, qkv, qkv, qkv, ps_t, ps, mk, rhs)


def _mixout_kernel(oh_ref, og_ref, hg_ref, gz_ref, x_ref, w_ref, nw_ref, ln_ref, o_ref, lhs_ref):
    def gated(o2_ref, gate_ref, wrow, col0):
        for h in range(N_HEADS):
            sl = slice(h * D_HEAD, (h + 1) * D_HEAD)
            o = o2_ref[0, :, sl] + o2_ref[1, :, sl]
            ms = jnp.mean(o * o, axis=-1, keepdims=True)
            gt = gate_ref[:, sl]
            y = o * lax.rsqrt(ms + NORM_EPS) * wrow * (gt * _sigmoid(gt))
            lhs_ref[:, col0 + h * D_HEAD:col0 + (h + 1) * D_HEAD] = y.astype(BF16)

    gated(oh_ref, hg_ref, nw_ref[0:1, :], 0)
    gated(og_ref, gz_ref, nw_ref[1:2, :], SEG)
    hmix = _dot(lhs_ref[...], w_ref[...])
    y = DN_ALPHA * x_ref[...] + hmix
    o_ref[...] = _layernorm_rows(y, ln_ref[0:1, :], ln_ref[1:2, :])


def _mixout(oh, og, proj, x, w_out_b, nw, ln, tm):
    n = x.shape[0]
    return pl.pallas_call(
        _mixout_kernel,
        grid=(n // tm,),
        in_specs=[
            pl.BlockSpec((2, tm, SEG), lambda i: (0, i, 0)),
            pl.BlockSpec((2, tm, SEG), lambda i: (0, i, 0)),
            pl.BlockSpec((tm, SEG), lambda i: (i, 4)),
            pl.BlockSpec((tm, SEG), lambda i: (i, 8)),
            pl.BlockSpec((tm, D_MODEL), lambda i: (i, 0)),
            pl.BlockSpec((2 * SEG, D_MODEL), lambda i: (0, 0)),
            pl.BlockSpec((SUBLANES, D_HEAD), lambda i: (0, 0)),
            pl.BlockSpec((SUBLANES, D_MODEL), lambda i: (0, 0)),
        ],
        out_specs=pl.BlockSpec((tm, D_MODEL), lambda i: (i, 0)),
        out_shape=jax.ShapeDtypeStruct((n, D_MODEL), F32),
        scratch_shapes=[pltpu.VMEM((tm, 2 * SEG), BF16)],
        compiler_params=_cparams(("parallel",)),
        name="mixer_out_ln1",
    )(oh, og, proj, proj, x, w_out_b, nw, ln)


def _attn_kernel(x_ref, kv_ref, wq_ref, wo_ref, ln_ref, wr_ref, br_ref, o_ref, r_ref, att_ref):
    x1 = x_ref[...]
    q = _dot(x1.astype(BF16), wq_ref[...])
    sc = XA_HEAD_DIM ** -0.5
    for h in range(XA_HEADS):
        sl = slice(h * XA_HEAD_DIM, (h + 1) * XA_HEAD_DIM)
        kh = kv_ref[:, sl]
        vh = kv_ref[:, D_MODEL + h * XA_HEAD_DIM:D_MODEL + (h + 1) * XA_HEAD_DIM]
        s = _dot_nt(q[:, sl].astype(BF16), kh) * sc
        s = s - jnp.max(s, axis=-1, keepdims=True)
        p = jnp.exp(s)
        p = p / jnp.sum(p, axis=-1, keepdims=True)
        att_ref[:, sl] = _dot(p.astype(BF16), vh).astype(BF16)
    hx = _dot(att_ref[...], wo_ref[...])
    x2 = _layernorm_rows(DN_ALPHA * x1 + hx, ln_ref[0:1, :], ln_ref[1:2, :])
    o_ref[...] = x2

    x_hi, x_mid = _split2(x2)
    lg2 = _dot(x_hi, wr_ref[...])
    lg = lg2[:, :LANES] + lg2[:, LANES:] + _dot(x_mid, wr_ref[:, :LANES]) + br_ref[0:1, :]
    tm = lg.shape[0]
    lane_i = lax.broadcasted_iota(jnp.int32, (tm, LANES), 1)
    lane = lane_i.astype(F32)
    grp_of_lane = (lane_i >> 3).astype(F32)
    neg = jnp.float32(-1e30)
    big = jnp.float32(LANES)
    is_g = (lane_i >= N_EXPERTS) & (lane_i < N_EXPERTS + N_GROUPS)
    gl = jnp.where(is_g, lg, neg)
    gmax = jnp.max(gl, axis=-1, keepdims=True)
    gsel = jnp.min(jnp.where(gl == gmax, lane, big), axis=-1, keepdims=True) - N_EXPERTS
    p_group = 1.0 / jnp.sum(jnp.where(is_g, jnp.exp(gl - gmax), 0.0), axis=-1, keepdims=True)
    in_grp = (lane_i < N_EXPERTS) & (grp_of_lane == gsel)
    el = jnp.where(in_grp, lg, neg)
    m1 = jnp.max(el, axis=-1, keepdims=True)
    i1 = jnp.min(jnp.where(el == m1, lane, big), axis=-1, keepdims=True)
    el2 = jnp.where(lane == i1, neg, el)
    m2 = jnp.max(el2, axis=-1, keepdims=True)
    i2 = jnp.min(jnp.where(el2 == m2, lane, big), axis=-1, keepdims=True)
    e2 = jnp.exp(m2 - m1)
    g1 = p_group / (1.0 + e2)
    g2 = p_group * e2 / (1.0 + e2)
    r = jnp.where(lane_i == 0, i1, 0.0)
    r = jnp.where(lane_i == 1, i2, r)
    r = jnp.where(lane_i == 2, g1, r)
    r = jnp.where(lane_i == 3, g2, r)
    r_ref[...] = r


def _attn(x1, kv, wq_b, wo_b, ln, wr, br, nb, t, tm):
    n = nb * t
    nblk = t // tm
    n_mem = kv.shape[0] // nb
    return pl.pallas_call(
        _attn_kernel,
        grid=(nb, nblk),
        in_specs=[
            pl.BlockSpec((tm, D_MODEL), lambda b, i: (b * nblk + i, 0)),
            pl.BlockSpec((n_mem, 2 * D_MODEL), lambda b, i: (b, 0)),
            pl.BlockSpec((D_MODEL, D_MODEL), lambda b, i: (0, 0)),
            pl.BlockSpec((D_MODEL, D_MODEL), lambda b, i: (0, 0)),
            pl.BlockSpec((SUBLANES, D_MODEL), lambda b, i: (0, 0)),
            pl.BlockSpec((D_MODEL, 2 * LANES), lambda b, i: (0, 0)),
            pl.BlockSpec((SUBLANES, LANES), lambda b, i: (0, 0)),
        ],
        out_specs=[pl.BlockSpec((tm, D_MODEL), lambda b, i: (b * nblk + i, 0)),
                   pl.BlockSpec((tm, LANES), lambda b, i: (b * nblk + i, 0))],
        out_shape=[jax.ShapeDtypeStruct((n, D_MODEL), F32), jax.ShapeDtypeStruct((n, LANES), F32)],
        scratch_shapes=[pltpu.VMEM((tm, D_MODEL), BF16)],
        compiler_params=_cparams(("parallel", "arbitrary")),
        name="xattn_ln2_router",
    )(x1, kv, wq_b, wo_b, ln, wr, br)


def _moe_kernel(be_ref, tok_ref, x_hbm, wg_ref, wu_ref, wd_ref, o_ref, xbuf, sem, *, tb):
    del be_ref

    def row_copy(r):
        return pltpu.make_async_copy(x_hbm.at[pl.ds(tok_ref[0, r], 1)], xbuf.at[pl.ds(r, 1)], sem)

    def start(r, carry):
        row_copy(r).start()
        return carry

    def wait(r, carry):
        row_copy(r).wait()
        return carry

    lax.fori_loop(0, tb, start, 0)
    lax.fori_loop(0, tb, wait, 0)
    xb = xbuf[...].astype(BF16)
    h1 = _dot(xb, wg_ref[...])
    h2 = _dot(xb, wu_ref[...])
    hdn = (h1 * _sigmoid(h1) * h2).astype(BF16)
    o_ref[...] = _dot(hdn, wd_ref[...])


def _moe(x2, block_e, slot_tok, wg_b, wu_b, wd_b):
    tb = MOE_TB
    cap = slot_tok.shape[0]
    n_blocks = cap // tb
    grid_spec = pltpu.PrefetchScalarGridSpec(
        num_scalar_prefetch=1,
        grid=(n_blocks,),
        in_specs=[
            pl.BlockSpec((None, 1, tb), lambda b, be: (b, 0, 0), memory_space=pltpu.SMEM),
            pl.BlockSpec(memory_space=pl.ANY),
            pl.BlockSpec((None, D_MODEL, D_EXPERT), lambda b, be: (be[b], 0, 0)),
            pl.BlockSpec((None, D_MODEL, D_EXPERT), lambda b, be: (be[b], 0, 0)),
            pl.BlockSpec((None, D_EXPERT, D_MODEL), lambda b, be: (be[b], 0, 0)),
        ],
        out_specs=pl.BlockSpec((tb, D_MODEL), lambda b, be: (b, 0)),
        scratch_shapes=[pltpu.VMEM((tb, D_MODEL), F32), pltpu.SemaphoreType.DMA(())],
    )
    return pl.pallas_call(
        functools.partial(_moe_kernel, tb=tb),
        grid_spec=grid_spec,
        out_shape=jax.ShapeDtypeStruct((cap, D_MODEL), F32),
        compiler_params=_cparams(("arbitrary",)),
        name="moe_grouped_mlp",
    )(block_e, slot_tok.reshape(n_blocks, 1, tb), x2, wg_b, wu_b, wd_b)


def _combine_kernel(pos_ref, y_hbm, x_ref, r_ref, ln_ref, o_ref, ybuf, sem, *, tm):
    def row_copy(r):
        return pltpu.make_async_copy(y_hbm.at[pl.ds(pos_ref[0, r], 1)], ybuf.at[pl.ds(r, 1)], sem)

    def start(r, carry):
        row_copy(r).start()
        return carry

    def wait(r, carry):
        row_copy(r).wait()
        return carry

    lax.fori_loop(0, 2 * tm, start, 0)
    lax.fori_loop(0, 2 * tm, wait, 0)
    r = r_ref[...]
    y = r[:, 2:3] * ybuf[0:tm, :] + r[:, 3:4] * ybuf[tm:2 * tm, :]
    o_ref[...] = _layernorm_rows(DN_ALPHA * x_ref[...] + y, ln_ref[0:1, :], ln_ref[1:2, :])


def _combine(y_slots, pos, x2, r, ln, tm):
    n = x2.shape[0]
    grid_spec = pltpu.PrefetchScalarGridSpec(
        num_scalar_prefetch=0,
        grid=(n // tm,),
        in_specs=[
            pl.BlockSpec((None, 1, 2 * tm), lambda i: (i, 0, 0), memory_space=pltpu.SMEM),
            pl.BlockSpec(memory_space=pl.ANY),
            pl.BlockSpec((tm, D_MODEL), lambda i: (i, 0)),
            pl.BlockSpec((tm, LANES), lambda i: (i, 0)),
            pl.BlockSpec((SUBLANES, D_MODEL), lambda i: (0, 0)),
        ],
        out_specs=pl.BlockSpec((tm, D_MODEL), lambda i: (i, 0)),
        scratch_shapes=[pltpu.VMEM((2 * tm, D_MODEL), F32), pltpu.SemaphoreType.DMA(())],
    )
    return pl.pallas_call(
        functools.partial(_combine_kernel, tm=tm),
        grid_spec=grid_spec,
        out_shape=jax.ShapeDtypeStruct((n, D_MODEL), F32),
        compiler_params=_cparams(("arbitrary",)),
        name="moe_combine_ln3",
    )(pos, y_slots, x2, r, ln)


def _route_slots(e1, e2, tb, tm):
    n = e1.shape[0]
    flat_e = jnp.stack([e1, e2], axis=1).reshape(-1)
    n_assign = 2 * n
    order = jnp.argsort(flat_e, stable=True)
    sorted_e = flat_e[order]
    counts = jnp.zeros((N_EXPERTS,), jnp.int32).at[flat_e].add(1)
    padded = (counts + tb - 1) // tb * tb
    pad_end = jnp.cumsum(padded)
    pad_start = pad_end - padded
    start = jnp.cumsum(counts) - counts
    dest = pad_start[sorted_e] + jnp.arange(n_assign, dtype=jnp.int32) - start[sorted_e]
    n_blocks = -(-n_assign // tb) + N_EXPERTS
    cap = n_blocks * tb
    slot_tok = jnp.zeros((cap,), jnp.int32).at[dest].set((order // 2).astype(jnp.int32))
    block_e = jnp.minimum(jnp.searchsorted(pad_end, jnp.arange(n_blocks, dtype=jnp.int32) * tb, side='right'),
                          N_EXPERTS - 1).astype(jnp.int32)
    slot_of = jnp.zeros((n_assign,), jnp.int32).at[order].set(dest.astype(jnp.int32))
    pos = slot_of.reshape(n // tm, tm, 2).transpose(0, 2, 1).reshape(n // tm, 1, 2 * tm)
    return block_e, slot_tok, pos


def _pad_rows(a, rows=SUBLANES):
    return jnp.concatenate([a, jnp.zeros((rows - a.shape[0],) + a.shape[1:], a.dtype)], axis=0)


def _layer(x, mem, w_in, hgrn_lb, hgrn_norm_w, gdn_conv_w, gdn_a_log, gdn_dt_bias, gdn_norm_w, w_out,
           ln1_g, ln1_b, xa_w_q, xa_w_kv, xa_w_o, ln2_g, ln2_b, moe_w_group, moe_b_group,
           moe_w_expert, moe_b_expert, moe_w_gate, moe_w_up, moe_w_down, ln3_g, ln3_b,
           *, tm_mm=1024, tbk=512, tm_row=256):
    nb, t, _ = x.shape
    n = nb * t
    xf = x.reshape(n, D_MODEL)
    layer = 0

    w_in_b = w_in[layer].astype(BF16)
    n_main = 9 * SEG
    w_small = jnp.concatenate([w_in_b[:, n_main:], jnp.zeros((D_MODEL, LANES - 4 * N_HEADS), BF16)], axis=1)
    xb = xf.astype(BF16)
    proj = _matmul(xb, w_in_b[:, :n_main], tm_mm, SEG, F32)
    ps = _matmul(xb, w_small, tm_mm, LANES, F32)

    lb2 = jnp.cumsum(jax.nn.softmax(hgrn_lb.astype(F32), axis=0), axis=0)[layer]
    oh = _hgrn2(proj, lb2, nb, t, HG_TBK, HG_HPG)

    qkv = _gdn_prep(proj, gdn_conv_w[layer, :, 0, :].astype(F32), nb, t, tbk)
    coef = jnp.stack([-jnp.exp(gdn_a_log[layer].astype(F32)).reshape(-1),
                      gdn_dt_bias[layer].astype(F32).reshape(-1)], axis=0)
    og = _gdn(qkv, ps, ps.T, coef, nb, t, GD_TBK, GD_HPG)

    nw = _pad_rows(jnp.stack([hgrn_norm_w[layer], gdn_norm_w[layer]], axis=0).astype(F32))
    ln1 = _pad_rows(jnp.stack([ln1_g[layer], ln1_b[layer]], axis=0).astype(F32))
    x1 = _mixout(oh, og, proj, xf, w_out[layer].astype(BF16), nw, ln1, tm_row)

    n_mem = mem.shape[1]
    kv = _matmul(mem.reshape(nb * n_mem, D_MODEL).astype(BF16), xa_w_kv[layer].astype(BF16),
                 n_mem, 1024, BF16)
    ln2 = _pad_rows(jnp.stack([ln2_g[layer], ln2_b[layer]], axis=0).astype(F32))
    w_r = jnp.concatenate([moe_w_expert[layer], moe_w_group[layer],
                           jnp.zeros((D_MODEL, LANES - N_EXPERTS - N_GROUPS), F32)], axis=1).astype(F32)
    w_r_hi = w_r.astype(BF16)
    w_r_mid = (w_r - w_r_hi.astype(F32)).astype(BF16)
    wr = jnp.concatenate([w_r_hi, w_r_mid], axis=1)
    b_r = jnp.concatenate([moe_b_expert[layer], moe_b_group[layer],
                           jnp.zeros((LANES - N_EXPERTS - N_GROUPS,), F32)]).astype(F32)
    br = jnp.broadcast_to(b_r[None, :], (SUBLANES, LANES))
    x2, r = _attn(x1, kv, xa_w_q[layer].astype(BF16), xa_w_o[layer].astype(BF16), ln2, wr, br, nb, t, tm_row)

    e1 = r[:, 0].astype(jnp.int32)
    e2 = r[:, 1].astype(jnp.int32)
    block_e, slot_tok, pos = _route_slots(e1, e2, MOE_TB, tm_row)
    y_slots = _moe(x2, block_e, slot_tok, moe_w_gate[layer].astype(BF16),
                   moe_w_up[layer].astype(BF16), moe_w_down[layer].astype(BF16))

    ln3 = _pad_rows(jnp.stack([ln3_g[layer], ln3_b[layer]], axis=0).astype(F32))
    y = _combine(y_slots, pos, x2, r, ln3, tm_row)
    return y.reshape(nb, t, D_MODEL)


def kernel(x_prompt, x_sample, mem_prompt, mem_sample, w_in, hgrn_lb, hgrn_norm_w, gdn_conv_w, gdn_a_log, gdn_dt_bias, gdn_norm_w, w_out, ln1_g, ln1_b, xa_w_q, xa_w_kv, xa_w_o, ln2_g, ln2_b, moe_w_group, moe_b_group, moe_w_expert, moe_b_expert, moe_w_gate, moe_w_up, moe_w_down, ln3_g, ln3_b):
    assert x_prompt.shape[1] == x_sample.shape[1]
    bp = x_prompt.shape[0]
    x = jnp.concatenate([x_prompt, x_sample], axis=0)
    mem = jnp.concatenate([mem_prompt, mem_sample], axis=0)
    y = _layer(x, mem, w_in, hgrn_lb, hgrn_norm_w, gdn_conv_w, gdn_a_log, gdn_dt_bias, gdn_norm_w, w_out,
               ln1_g, ln1_b, xa_w_q, xa_w_kv, xa_w_o, ln2_g, ln2_b, moe_w_group, moe_b_group,
               moe_w_expert, moe_b_expert, moe_w_gate, moe_w_up, moe_w_down, ln3_g, ln3_b)
    return (y[:bp], y[bp:])
```

```python
import functools
import math

import numpy as np
import jax
import jax.numpy as jnp
from jax import lax
from jax.experimental import pallas as pl
from jax.experimental.pallas import tpu as pltpu

F32 = jnp.float32
BF16 = jnp.bfloat16

D_MODEL = 2048
N_HEADS = 8
D_HEAD = 128
SEG = N_HEADS * D_HEAD
CONV_K = 5
XA_HEADS = 4
XA_HEAD_DIM = D_MODEL // XA_HEADS
N_GROUPS = 4
EXPERTS_PER_GROUP = 8
N_EXPERTS = N_GROUPS * EXPERTS_PER_GROUP
D_EXPERT = 512
DN_ALPHA = 2.0 ** 0.25
LN_EPS = 1e-5
NORM_EPS = 1e-6

LANES = 128
SUBLANES = 8
VMEM_LIMIT = 56 * 1024 * 1024

HG_CHUNK = 128
HG_TBK = 256
HG_HPG = 4
HG_SMALL_LEVELS = 3
GD_CHUNK = 128
GD_TBK = 256
GD_HPG = 4
MOE_TB = 256


def _cparams(sem):
    return pltpu.CompilerParams(dimension_semantics=sem, vmem_limit_bytes=VMEM_LIMIT)


def _dot(a, b):
    return jnp.dot(a, b, preferred_element_type=F32)


def _dot_nt(a, b):
    return lax.dot_general(a, b, (((1,), (1,)), ((), ())), preferred_element_type=F32)


def _dot_tn(a, b):
    return lax.dot_general(a, b, (((0,), (0,)), ((), ())), preferred_element_type=F32)


def _split2(x):
    hi = x.astype(BF16)
    mid = (x - hi.astype(F32)).astype(BF16)
    return hi, mid


def _sigmoid(x):
    return 1.0 / (1.0 + jnp.exp(-x))


def _layernorm_rows(y, g, b):
    mu = jnp.mean(y, axis=-1, keepdims=True)
    yc = y - mu
    var = jnp.mean(yc * yc, axis=-1, keepdims=True)
    return yc * lax.rsqrt(var + LN_EPS) * g + b


def _mm_kernel(x_ref, w_ref, o_ref):
    o_ref[...] = _dot(x_ref[...], w_ref[...]).astype(o_ref.dtype)


def _matmul(x, w, tm, tn, out_dtype):
    m, k = x.shape
    n = w.shape[1]
    return pl.pallas_call(
        _mm_kernel,
        grid=(m // tm, n // tn),
        in_specs=[pl.BlockSpec((tm, k), lambda i, j: (i, 0)),
                  pl.BlockSpec((k, tn), lambda i, j: (0, j))],
        out_specs=pl.BlockSpec((tm, tn), lambda i, j: (i, j)),
        out_shape=jax.ShapeDtypeStruct((m, n), out_dtype),
        compiler_params=_cparams(("parallel", "arbitrary")),
        name="dense_matmul",
    )(x, w)


def _hgrn_constants(c):
    nl = int(math.log2(c))
    t = np.arange(c)
    u = np.arange(c)
    blocks = []
    blocks.append((u[None, :] <= t[:, None]).astype(np.float32))
    masks, qrows = [], []
    for l in range(nl):
        cc = 1 << l
        m = (t // (2 * cc)) * (2 * cc) + cc
        upper = ((t >> l) & 1) == 1
        a = np.zeros((c, c), np.float32)
        for r in range(c):
            if upper[r]:
                a[r, m[r]:r + 1] = 1.0
            else:
                a[r, r + 1:m[r]] = 1.0
        if l < HG_SMALL_LEVELS:
            blocks.append(a)
        same = (t[:, None] >> (l + 1)) == (t[None, :] >> (l + 1))
        masks.append((same & upper[:, None] & (~upper)[None, :]).astype(np.float32))
        qrows.append(np.broadcast_to(upper[:, None], (c, D_HEAD)).astype(np.float32))
    blocks.append(np.ones((16, c), np.float32))
    a_f = np.concatenate(blocks, axis=0)
    m_f = np.stack(masks)
    q_f = np.stack(qrows)
    a_b = np.concatenate([b[::-1, ::-1] for b in blocks], axis=0)
    m_b = m_f[:, ::-1, ::-1]
    q_b = q_f[:, ::-1, :]
    return (np.stack([a_f, a_b]), np.stack([m_f, m_b]).copy(), np.stack([q_f, q_b]).copy(), nl)


def _hgrn_kernel(q_ref, f_ref, v_ref, lb_ref, a_ref, m_ref, qr_ref, o_ref, st_ref, *, c, nck, nl, scale, hpg):
    d = pl.program_id(0)
    i = pl.program_id(3)

    @pl.when(i == 0)
    def _():
        st_ref[...] = jnp.zeros_like(st_ref)

    probs = [(j, g) for j in range(nck) for g in range(hpg)]

    def rows(j):
        jj = jnp.where(d == 0, j, nck - 1 - j)
        return pl.ds(pl.multiple_of(jj * c, c), c)

    def cols(g):
        return slice(g * D_HEAD, (g + 1) * D_HEAD)

    def eblk(k, g2, rows=c):
        e2 = _dot(a_ref[pl.ds(k * c, rows), :], g2)
        return e2[:, :D_HEAD] + e2[:, D_HEAD:]

    def boundary(gcum, l):
        cc = 1 << l
        pieces = []
        for p in range(c // (2 * cc)):
            m = p * 2 * cc + cc
            rowv = jnp.where(d == 0, gcum[m - 1:m, :], gcum[m:m + 1, :])
            pieces.append(jnp.broadcast_to(rowv, (2 * cc, D_HEAD)))
        return pieces[0] if len(pieces) == 1 else jnp.concatenate(pieces, axis=0)

    qs, kk, vb, vv, g2, gcum, gtot = {}, {}, {}, {}, {}, {}, {}
    for (j, g) in probs:
        lb = lb_ref[g, 0:1, :]
        one_m_lb = 1.0 - lb
        fr = f_ref[rows(j), cols(g)]
        qs[j, g] = q_ref[rows(j), cols(g)] * scale
        v = v_ref[rows(j), cols(g)]
        vv[j, g] = v
        vb[j, g] = v.astype(BF16)
        gl = jnp.log(lb + one_m_lb * _sigmoid(fr))
        kk[j, g] = one_m_lb * _sigmoid(-fr)
        g_hi, g_mid = _split2(gl)
        g2[j, g] = jnp.concatenate([g_hi, g_mid], axis=1)
    for p in probs:
        gcum[p] = eblk(0, g2[p])
        gtot[p] = eblk(1 + HG_SMALL_LEVELS, g2[p], rows=16)[0:1, :]

    s = {}
    for l in range(nl):
        if l < HG_SMALL_LEVELS:
            e = {p: eblk(1 + l, g2[p]) for p in probs}
        else:
            e = {p: -jnp.abs(gcum[p] - boundary(gcum[p], l)) for p in probs}
        w = {p: (jnp.where(qr_ref[l] > 0.5, qs[p], kk[p]) * jnp.exp(e[p])).astype(BF16) for p in probs}
        sd = {p: _dot_nt(w[p], w[p]) for p in probs}
        for p in probs:
            s[p] = m_ref[l] * sd[p] if l == 0 else s[p] + m_ref[l] * sd[p]

    o_in, qd, upd, dec = {}, {}, {}, {}
    for p in probs:
        dg = jnp.sum(qs[p] * kk[p], axis=-1, keepdims=True)
        o_in[p] = _dot(s[p].astype(BF16), vb[p]) + dg * vv[p]
        qd[p] = (qs[p] * jnp.exp(gcum[p])).astype(BF16)
        kd = (kk[p] * jnp.exp(gtot[p] - gcum[p])).astype(BF16)
        upd[p] = _dot_tn(vb[p], kd)
        dec[p] = jnp.exp(gtot[p])

    st = [st_ref[g] for g in range(hpg)]
    for j in range(nck):
        for g in range(hpg):
            p = (j, g)
            o_ref[rows(j), cols(g)] = o_in[p] + _dot_nt(qd[p], st[g].astype(BF16))
            st[g] = st[g] * dec[p] + upd[p]
    for g in range(hpg):
        st_ref[g] = st[g]


def _hgrn2(proj, lb2, nb, t, tbk, hpg):
    c = HG_CHUNK
    n = nb * t
    nblk = t // tbk
    a_np, m_np, q_np, nl = _hgrn_constants(c)
    a_all = jnp.asarray(a_np, BF16)
    masks = jnp.asarray(m_np, F32)
    qrows = jnp.asarray(q_np, F32)
    lb8 = jnp.broadcast_to(lb2.reshape(2, N_HEADS, 1, D_HEAD), (2, N_HEADS, SUBLANES, D_HEAD))
    ngrp = N_HEADS // hpg
    wb = hpg * D_HEAD

    def row(d, b, i):
        return b * nblk + jnp.where(d == 0, i, nblk - 1 - i)

    kern = functools.partial(_hgrn_kernel, c=c, nck=tbk // c, nl=nl, scale=D_HEAD ** -0.5, hpg=hpg)
    na = a_np.shape[1]
    return pl.pallas_call(
        kern,
        grid=(2, nb, ngrp, nblk),
        in_specs=[
            pl.BlockSpec((tbk, wb), lambda d, b, h, i: (row(d, b, i), h)),
            pl.BlockSpec((tbk, wb), lambda d, b, h, i: (row(d, b, i), (1 + d) * ngrp + h)),
            pl.BlockSpec((tbk, wb), lambda d, b, h, i: (row(d, b, i), 3 * ngrp + h)),
            pl.BlockSpec((None, hpg, SUBLANES, D_HEAD), lambda d, b, h, i: (d, h, 0, 0)),
            pl.BlockSpec((None, na, c), lambda d, b, h, i: (d, 0, 0)),
            pl.BlockSpec((None, nl, c, c), lambda d, b, h, i: (d, 0, 0, 0)),
            pl.BlockSpec((None, nl, c, D_HEAD), lambda d, b, h, i: (d, 0, 0, 0)),
        ],
        out_specs=pl.BlockSpec((None, tbk, wb), lambda d, b, h, i: (d, row(d, b, i), h)),
        out_shape=jax.ShapeDtypeStruct((2, n, SEG), F32),
        scratch_shapes=[pltpu.VMEM((hpg, D_HEAD, D_HEAD), F32)],
        compiler_params=_cparams(("parallel", "parallel", "parallel", "arbitrary")),
        name="hgrn2_bidir",
    )(proj, proj, proj, lb8, a_all, masks, qrows)


def _gdn_prep_kernel(prev_ref, cur_ref, nxt_ref, w_ref, o_ref, *, tbk, scale):
    i = pl.program_id(1)
    s = pl.program_id(2)
    nblk = pl.num_programs(1)
    prev = jnp.where(i == 0, 0.0, prev_ref[...])
    nxt = jnp.where(i == nblk - 1, 0.0, nxt_ref[...])
    full = jnp.concatenate([prev, cur_ref[...], nxt], axis=0)
    rows = tbk + 2 * SUBLANES
    acc = jnp.zeros((tbk, SEG), F32)
    for j in range(CONV_K):
        shift = (CONV_K // 2 - j) % rows
        sh = full if shift == 0 else pltpu.roll(full, shift, axis=0)
        acc = acc + sh[SUBLANES:SUBLANES + tbk, :] * w_ref[j:j + 1, :]
    y = acc * _sigmoid(acc)
    fac_all = jnp.where(s == 0, scale, 1.0)
    for h in range(N_HEADS):
        ys = y[:, h * D_HEAD:(h + 1) * D_HEAD]
        ss = jnp.sum(ys * ys, axis=-1, keepdims=True)
        fac = jnp.where(s < 2, lax.rsqrt(ss + NORM_EPS), 1.0) * fac_all
        o_ref[:, h * D_HEAD:(h + 1) * D_HEAD] = ys * fac


def _gdn_prep(proj, conv_w, nb, t, tbk):
    n = nb * t
    nblk = t // tbk
    hb = tbk // SUBLANES
    w8 = jnp.concatenate([conv_w, jnp.zeros((SUBLANES - CONV_K, 3 * SEG), F32)], axis=0)
    last = n // SUBLANES - 1
    kern = functools.partial(_gdn_prep_kernel, tbk=tbk, scale=D_HEAD ** -0.5)
    return pl.pallas_call(
        kern,
        grid=(nb, nblk, 3),
        in_specs=[
            pl.BlockSpec((SUBLANES, SEG), lambda b, i, s: (jnp.maximum((b * nblk + i) * hb - 1, 0), 5 + s)),
            pl.BlockSpec((tbk, SEG), lambda b, i, s: (b * nblk + i, 5 + s)),
            pl.BlockSpec((SUBLANES, SEG), lambda b, i, s: (jnp.minimum((b * nblk + i + 1) * hb, last), 5 + s)),
            pl.BlockSpec((SUBLANES, SEG), lambda b, i, s: (0, s)),
        ],
        out_specs=pl.BlockSpec((None, tbk, SEG), lambda b, i, s: (s, b * nblk + i, 0)),
        out_shape=jax.ShapeDtypeStruct((3, n, SEG), F32),
        compiler_params=_cparams(("parallel", "parallel", "arbitrary")),
        name="gdn_conv_prep",
    )(proj, proj, proj, w8)


def _gdn_constants(c):
    nl = int(math.log2(c))
    t = np.arange(c)
    incl = (t[None, :] <= t[:, None]).astype(np.float32)
    strict = (t[None, :] < t[:, None]).astype(np.float32)
    after = (t[:, None] > t[None, :]).astype(np.float32)
    rhs_f = np.concatenate([np.ones((c, LANES), np.float32), after], axis=1)
    lvl = []
    for l in range(nl):
        upper = ((t >> l) & 1) == 1
        same = (t[:, None] >> (l + 1)) == (t[None, :] >> (l + 1))
        lvl.append((same & upper[:, None] & (~upper)[None, :]).astype(np.float32))
    lvl = np.stack(lvl)
    mk_f = np.concatenate([incl[None], strict[None], lvl], axis=0)
    mk_b = mk_f[:, ::-1, ::-1]
    rhs_b = np.concatenate([np.ones((c, LANES), np.float32), after[::-1, ::-1]], axis=1)
    return np.stack([mk_f, mk_b]).copy(), np.stack([rhs_f, rhs_b]).copy(), nl


def _gdn_kernel(coef_ref, q_ref, k_ref, v_ref, arow_ref, ps_ref, mk_ref, rhs_ref, o_ref, st_ref, *, c, nck, nl, hpg):
    d = pl.program_id(0)
    hg = pl.program_id(2)
    i = pl.program_id(3)

    @pl.when(i == 0)
    def _():
        st_ref[...] = jnp.zeros_like(st_ref)

    lane = lax.broadcasted_iota(jnp.int32, (c, LANES), 1)
    incl = mk_ref[0]
    strict = mk_ref[1]
    eye = incl - strict
    rhs = rhs_ref[...]
    probs = [(j, g) for j in range(nck) for g in range(hpg)]

    def rows(j):
        jj = jnp.where(d == 0, j, nck - 1 - j)
        return pl.ds(pl.multiple_of(jj * c, c), c)

    def cols(g):
        return slice(g * D_HEAD, (g + 1) * D_HEAD)

    qn, kn, vv, beta, gcum, dec, etot = {}, {}, {}, {}, {}, {}, {}
    for (j, g) in probs:
        h = hg * hpg + g
        neg_ea = coef_ref[0, d * N_HEADS + h]
        dtb = coef_ref[1, d * N_HEADS + h]
        a_blk = arow_ref[pl.ds(h, 1), :]
        a_row = jnp.where(d == 0, a_blk[:, j * c:(j + 1) * c], a_blk[:, (nck - 1 - j) * c:(nck - j) * c])
        z = a_row + dtb
        g_row = neg_ea * (jnp.maximum(z, 0.0) + jnp.log(1.0 + jnp.exp(-jnp.abs(z))))
        beta_lane = 2 * N_HEADS + d * N_HEADS + h
        b_col = jnp.sum(jnp.where(lane == beta_lane, ps_ref[rows(j), :], 0.0), axis=-1, keepdims=True)
        beta[j, g] = _sigmoid(b_col)
        lg = incl * g_row
        lg_hi, lg_mid = _split2(lg)
        out = _dot(jnp.concatenate([lg_hi, lg_mid], axis=0), rhs)
        out = out[:c] + out[c:]
        gcum[j, g] = out[:, :LANES]
        dmat = out[:, LANES:]
        dec[j, g] = jnp.exp(jnp.minimum(dmat, 0.0)) * incl
        qn[j, g] = q_ref[rows(j), cols(g)]
        kn[j, g] = k_ref[rows(j), cols(g)]
        vv[j, g] = v_ref[rows(j), cols(g)]

    kb, knb, a_dec, tinv = {}, {}, {}, {}
    for p in probs:
        kb[p] = kn[p] * beta[p]
        knb[p] = kn[p].astype(BF16)
        a_dec[p] = _dot_nt(kb[p].astype(BF16), knb[p]) * dec[p]
        tinv[p] = eye - a_dec[p] * mk_ref[2]
    for l in range(1, nl):
        ml = mk_ref[2 + l]
        tb = {p: tinv[p].astype(BF16) for p in probs}
        ta = {p: _dot(tb[p], (a_dec[p] * ml).astype(BF16)).astype(BF16) for p in probs}
        for p in probs:
            tinv[p] = tinv[p] - _dot(ta[p], tb[p])

    u, w, qk, qe, ke = {}, {}, {}, {}, {}
    for p in probs:
        gtot = jnp.where(d == 0, gcum[p][c - 1:c, :], gcum[p][0:1, :])
        eg = jnp.exp(gcum[p])
        etot[p] = jnp.exp(gtot)
        uw = _dot(tinv[p].astype(BF16), jnp.concatenate([vv[p] * beta[p], kb[p] * eg], axis=1).astype(BF16))
        u[p] = uw[:, :D_HEAD]
        w[p] = uw[:, D_HEAD:].astype(BF16)
        qk[p] = (_dot_nt(qn[p].astype(BF16), knb[p]) * dec[p]).astype(BF16)
        qe[p] = (qn[p] * eg).astype(BF16)
        ke[p] = (kn[p] * jnp.exp(gtot - gcum[p])).astype(BF16)

    st = [st_ref[g] for g in range(hpg)]
    for j in range(nck):
        stb = [st[g].astype(BF16) for g in range(hpg)]
        ws = [_dot(w[j, g], stb[g]) for g in range(hpg)]
        qs = [_dot(qe[j, g], stb[g]) for g in range(hpg)]
        vnb = [(u[j, g] - ws[g]).astype(BF16) for g in range(hpg)]
        for g in range(hpg):
            o_ref[rows(j), cols(g)] = qs[g] + _dot(qk[j, g], vnb[g])
        for g in range(hpg):
            st[g] = st[g] * etot[j, g] + _dot_tn(ke[j, g], vnb[g])
    for g in range(hpg):
        st_ref[g] = st[g]


def _gdn(qkv, ps, ps_t, coef, nb, t, tbk, hpg):
    c = GD_CHUNK
    n = nb * t
    nblk = t // tbk
    mk_np, rhs_np, nl = _gdn_constants(c)
    mk = jnp.asarray(mk_np, F32)
    rhs = jnp.asarray(rhs_np, BF16)
    wb = hpg * D_HEAD

    def row(d, b, i):
        return b * nblk + jnp.where(d == 0, i, nblk - 1 - i)

    kern = functools.partial(_gdn_kernel, c=c, nck=tbk // c, nl=nl, hpg=hpg)
    grid_spec = pltpu.PrefetchScalarGridSpec(
        num_scalar_prefetch=0,
        grid=(2, nb, N_HEADS // hpg, nblk),
        in_specs=[
            pl.BlockSpec(memory_space=pltpu.SMEM),
            pl.BlockSpec((None, tbk, wb), lambda d, b, h, i: (0, row(d, b, i), h)),
            pl.BlockSpec((None, tbk, wb), lambda d, b, h, i: (1, row(d, b, i), h)),
            pl.BlockSpec((None, tbk, wb), lambda d, b, h, i: (2, row(d, b, i), h)),
            pl.BlockSpec((SUBLANES, tbk), lambda d, b, h, i: (d, row(d, b, i))),
            pl.BlockSpec((tbk, LANES), lambda d, b, h, i: (row(d, b, i), 0)),
            pl.BlockSpec((None, 2 + nl, c, c), lambda d, b, h, i: (d, 0, 0, 0)),
            pl.BlockSpec((None, c, c + LANES), lambda d, b, h, i: (d, 0, 0)),
        ],
        out_specs=pl.BlockSpec((None, tbk, wb), lambda d, b, h, i: (d, row(d, b, i), h)),
        scratch_shapes=[pltpu.VMEM((hpg, D_HEAD, D_HEAD), F32)],
    )
    return pl.pallas_call(
        kern,
        grid_spec=grid_spec,
        out_shape=jax.ShapeDtypeStruct((2, n, SEG), F32),
        compiler_params=_cparams(("parallel", "parallel", "parallel", "arbitrary")),
        name="gdn_bidir",
    )(coef, qkv, qkv, qkv, ps_t, ps, mk, rhs)


def _mixout_kernel(oh_ref, og_ref, hg_ref, gz_ref, x_ref, w_ref, nw_ref, ln_ref, o_ref, lhs_ref):
    def gated(o2_ref, gate_ref, wrow, col0):
        for h in range(N_HEADS):
            sl = slice(h * D_HEAD, (h + 1) * D_HEAD)
            o = o2_ref[0, :, sl] + o2_ref[1, :, sl]
            ms = jnp.mean(o * o, axis=-1, keepdims=True)
            gt = gate_ref[:, sl]
            y = o * lax.rsqrt(ms + NORM_EPS) * wrow * (gt * _sigmoid(gt))
            lhs_ref[:, col0 + h * D_HEAD:col0 + (h + 1) * D_HEAD] = y.astype(BF16)

    gated(oh_ref, hg_ref, nw_ref[0:1, :], 0)
    gated(og_ref, gz_ref, nw_ref[1:2, :], SEG)
    hmix = _dot(lhs_ref[...], w_ref[...])
    y = DN_ALPHA * x_ref[...] + hmix
    o_ref[...] = _layernorm_rows(y, ln_ref[0:1, :], ln_ref[1:2, :])


def _mixout(oh, og, proj, x, w_out_b, nw, ln, tm):
    n = x.shape[0]
    return pl.pallas_call(
        _mixout_kernel,
        grid=(n // tm,),
        in_specs=[
            pl.BlockSpec((2, tm, SEG), lambda i: (0, i, 0)),
            pl.BlockSpec((2, tm, SEG), lambda i: (0, i, 0)),
            pl.BlockSpec((tm, SEG), lambda i: (i, 4)),
            pl.BlockSpec((tm, SEG), lambda i: (i, 8)),
            pl.BlockSpec((tm, D_MODEL), lambda i: (i, 0)),
            pl.BlockSpec((2 * SEG, D_MODEL), lambda i: (0, 0)),
            pl.BlockSpec((SUBLANES, D_HEAD), lambda i: (0, 0)),
            pl.BlockSpec((SUBLANES, D_MODEL), lambda i: (0, 0)),
        ],
        out_specs=pl.BlockSpec((tm, D_MODEL), lambda i: (i, 0)),
        out_shape=jax.ShapeDtypeStruct((n, D_MODEL), F32),
        scratch_shapes=[pltpu.VMEM((tm, 2 * SEG), BF16)],
        compiler_params=_cparams(("parallel",)),
        name="mixer_out_ln1",
    )(oh, og, proj, proj, x, w_out_b, nw, ln)


def _attn_kernel(x_ref, kv_ref, wq_ref, wo_ref, ln_ref, wr_ref, br_ref, o_ref, r_ref, att_ref):
    x1 = x_ref[...]
    q = _dot(x1.astype(BF16), wq_ref[...])
    sc = XA_HEAD_DIM ** -0.5
    for h in range(XA_HEADS):
        sl = slice(h * XA_HEAD_DIM, (h + 1) * XA_HEAD_DIM)
        kh = kv_ref[:, sl]
        vh = kv_ref[:, D_MODEL + h * XA_HEAD_DIM:D_MODEL + (h + 1) * XA_HEAD_DIM]
        s = _dot_nt(q[:, sl].astype(BF16), kh) * sc
        s = s - jnp.max(s, axis=-1, keepdims=True)
        p = jnp.exp(s)
        p = p / jnp.sum(p, axis=-1, keepdims=True)
        att_ref[:, sl] = _dot(p.astype(BF16), vh).astype(BF16)
    hx = _dot(att_ref[...], wo_ref[...])
    x2 = _layernorm_rows(DN_ALPHA * x1 + hx, ln_ref[0:1, :], ln_ref[1:2, :])
    o_ref[...] = x2

    x_hi, x_mid = _split2(x2)
    lg2 = _dot(x_hi, wr_ref[...])
    lg = lg2[:, :LANES] + lg2[:, LANES:] + _dot(x_mid, wr_ref[:, :LANES]) + br_ref[0:1, :]
    tm = lg.shape[0]
    lane_i = lax.broadcasted_iota(jnp.int32, (tm, LANES), 1)
    lane = lane_i.astype(F32)
    grp_of_lane = (lane_i >> 3).astype(F32)
    neg = jnp.float32(-1e30)
    big = jnp.float32(LANES)
    is_g = (lane_i >= N_EXPERTS) & (lane_i < N_EXPERTS + N_GROUPS)
    gl = jnp.where(is_g, lg, neg)
    gmax = jnp.max(gl, axis=-1, keepdims=True)
    gsel = jnp.min(jnp.where(gl == gmax, lane, big), axis=-1, keepdims=True) - N_EXPERTS
    p_group = 1.0 / jnp.sum(jnp.where(is_g, jnp.exp(gl - gmax), 0.0), axis=-1, keepdims=True)
    in_grp = (lane_i < N_EXPERTS) & (grp_of_lane == gsel)
    el = jnp.where(in_grp, lg, neg)
    m1 = jnp.max(el, axis=-1, keepdims=True)
    i1 = jnp.min(jnp.where(el == m1, lane, big), axis=-1, keepdims=True)
    el2 = jnp.where(lane == i1, neg, el)
    m2 = jnp.max(el2, axis=-1, keepdims=True)
    i2 = jnp.min(jnp.where(el2 == m2, lane, big), axis=-1, keepdims=True)
    e2 = jnp.exp(m2 - m1)
    g1 = p_group / (1.0 + e2)
    g2 = p_group * e2 / (1.0 + e2)
    r = jnp.where(lane_i == 0, i1, 0.0)
    r = jnp.where(lane_i == 1, i2, r)
    r = jnp.where(lane_i == 2, g1, r)
    r = jnp.where(lane_i == 3, g2, r)
    r_ref[...] = r


def _attn(x1, kv, wq_b, wo_b, ln, wr, br, nb, t, tm):
    n = nb * t
    nblk = t // tm
    n_mem = kv.shape[0] // nb
    return pl.pallas_call(
        _attn_kernel,
        grid=(nb, nblk),
        in_specs=[
            pl.BlockSpec((tm, D_MODEL), lambda b, i: (b * nblk + i, 0)),
            pl.BlockSpec((n_mem, 2 * D_MODEL), lambda b, i: (b, 0)),
            pl.BlockSpec((D_MODEL, D_MODEL), lambda b, i: (0, 0)),
            pl.BlockSpec((D_MODEL, D_MODEL), lambda b, i: (0, 0)),
            pl.BlockSpec((SUBLANES, D_MODEL), lambda b, i: (0, 0)),
            pl.BlockSpec((D_MODEL, 2 * LANES), lambda b, i: (0, 0)),
            pl.BlockSpec((SUBLANES, LANES), lambda b, i: (0, 0)),
        ],
        out_specs=[pl.BlockSpec((tm, D_MODEL), lambda b, i: (b * nblk + i, 0)),
                   pl.BlockSpec((tm, LANES), lambda b, i: (b * nblk + i, 0))],
        out_shape=[jax.ShapeDtypeStruct((n, D_MODEL), F32), jax.ShapeDtypeStruct((n, LANES), F32)],
        scratch_shapes=[pltpu.VMEM((tm, D_MODEL), BF16)],
        compiler_params=_cparams(("parallel", "arbitrary")),
        name="xattn_ln2_router",
    )(x1, kv, wq_b, wo_b, ln, wr, br)


DMA_UNROLL = 8


def _issue_row_gather(idx_ref, src_hbm, dst, sem, n_rows):
    def body(k, carry):
        for u in range(DMA_UNROLL):
            r = k * DMA_UNROLL + u
            pltpu.make_async_copy(src_hbm.at[pl.ds(idx_ref[0, r], 1)], dst.at[pl.ds(r, 1)], sem).start(priority=u % 2)
        return carry

    lax.fori_loop(0, n_rows // DMA_UNROLL, body, 0)


def _wait_row_gather(src_hbm, dst, sem, n_rows):
    pltpu.make_async_copy(src_hbm.at[pl.ds(0, n_rows)], dst, sem).wait()


def _moe_kernel(be_ref, nv_ref, tok_ref, tokn_ref, x_hbm, wg_ref, wu_ref, wd_ref, o_ref, xbuf, sem, *, tb):
    del be_ref
    b = pl.program_id(0)
    nb = pl.num_programs(0)
    slot = b % 2

    @pl.when((b == 0) & (nv_ref[0] > 0))
    def _():
        _issue_row_gather(tok_ref, x_hbm, xbuf.at[0], sem.at[0], tb)

    @pl.when((b + 1 < nb) & (nv_ref[jnp.minimum(b + 1, nb - 1)] > 0))
    def _():
        _issue_row_gather(tokn_ref, x_hbm, xbuf.at[1 - slot], sem.at[1 - slot], tb)

    @pl.when(nv_ref[b] > 0)
    def _():
        _wait_row_gather(x_hbm, xbuf.at[slot], sem.at[slot], tb)
        xb = xbuf[slot].astype(BF16)
        h1 = _dot(xb, wg_ref[...])
        h2 = _dot(xb, wu_ref[...])
        hdn = (h1 * _sigmoid(h1) * h2).astype(BF16)
        o_ref[...] = _dot(hdn, wd_ref[...])

    @pl.when(nv_ref[b] == 0)
    def _():
        o_ref[...] = jnp.zeros_like(o_ref)


def _moe(x2, block_e, block_nv, slot_tok, wg_b, wu_b, wd_b):
    tb = MOE_TB
    cap = slot_tok.shape[0]
    n_blocks = cap // tb
    tok3 = slot_tok.reshape(n_blocks, 1, tb)
    grid_spec = pltpu.PrefetchScalarGridSpec(
        num_scalar_prefetch=2,
        grid=(n_blocks,),
        in_specs=[
            pl.BlockSpec((None, 1, tb), lambda b, be, nv: (b, 0, 0), memory_space=pltpu.SMEM),
            pl.BlockSpec((None, 1, tb), lambda b, be, nv: (jnp.minimum(b + 1, n_blocks - 1), 0, 0),
                         memory_space=pltpu.SMEM),
            pl.BlockSpec(memory_space=pl.ANY),
            pl.BlockSpec((None, D_MODEL, D_EXPERT), lambda b, be, nv: (be[b], 0, 0)),
            pl.BlockSpec((None, D_MODEL, D_EXPERT), lambda b, be, nv: (be[b], 0, 0)),
            pl.BlockSpec((None, D_EXPERT, D_MODEL), lambda b, be, nv: (be[b], 0, 0)),
        ],
        out_specs=pl.BlockSpec((tb, D_MODEL), lambda b, be, nv: (b, 0)),
        scratch_shapes=[pltpu.VMEM((2, tb, D_MODEL), F32), pltpu.SemaphoreType.DMA((2,))],
    )
    return pl.pallas_call(
        functools.partial(_moe_kernel, tb=tb),
        grid_spec=grid_spec,
        out_shape=jax.ShapeDtypeStruct((cap, D_MODEL), F32),
        compiler_params=_cparams(("arbitrary",)),
        name="moe_grouped_mlp",
    )(block_e, block_nv, tok3, tok3, x2, wg_b, wu_b, wd_b)


def _combine_kernel(pos_ref, posn_ref, y_hbm, x_ref, r_ref, ln_ref, op_ref, os_ref, ybuf, sem, *, tm, npt):
    i = pl.program_id(0)
    nt = pl.num_programs(0)
    slot = i % 2

    @pl.when(i == 0)
    def _():
        _issue_row_gather(pos_ref, y_hbm, ybuf.at[0], sem.at[0], 2 * tm)

    @pl.when(i + 1 < nt)
    def _():
        _issue_row_gather(posn_ref, y_hbm, ybuf.at[1 - slot], sem.at[1 - slot], 2 * tm)

    _wait_row_gather(y_hbm, ybuf.at[slot], sem.at[slot], 2 * tm)
    r = r_ref[...]
    y = r[:, 2:3] * ybuf[slot, 0:tm, :] + r[:, 3:4] * ybuf[slot, tm:2 * tm, :]
    res = _layernorm_rows(DN_ALPHA * x_ref[...] + y, ln_ref[0:1, :], ln_ref[1:2, :])

    @pl.when(i < npt)
    def _():
        op_ref[...] = res

    @pl.when(i >= npt)
    def _():
        os_ref[...] = res


def _combine(y_slots, pos, x2, r, ln, tm, n_first):
    n = x2.shape[0]
    nt = n // tm
    npt = n_first // tm
    grid_spec = pltpu.PrefetchScalarGridSpec(
        num_scalar_prefetch=0,
        grid=(nt,),
        in_specs=[
            pl.BlockSpec((None, 1, 2 * tm), lambda i: (i, 0, 0), memory_space=pltpu.SMEM),
            pl.BlockSpec((None, 1, 2 * tm), lambda i: (jnp.minimum(i + 1, nt - 1), 0, 0), memory_space=pltpu.SMEM),
            pl.BlockSpec(memory_space=pl.ANY),
            pl.BlockSpec((tm, D_MODEL), lambda i: (i, 0)),
            pl.BlockSpec((tm, LANES), lambda i: (i, 0)),
            pl.BlockSpec((SUBLANES, D_MODEL), lambda i: (0, 0)),
        ],
        out_specs=[pl.BlockSpec((tm, D_MODEL), lambda i: (jnp.minimum(i, npt - 1), 0)),
                   pl.BlockSpec((tm, D_MODEL), lambda i: (jnp.maximum(i - npt, 0), 0))],
        scratch_shapes=[pltpu.VMEM((2, 2 * tm, D_MODEL), F32), pltpu.SemaphoreType.DMA((2,))],
    )
    return pl.pallas_call(
        functools.partial(_combine_kernel, tm=tm, npt=npt),
        grid_spec=grid_spec,
        out_shape=[jax.ShapeDtypeStruct((n_first, D_MODEL), F32), jax.ShapeDtypeStruct((n - n_first, D_MODEL), F32)],
        compiler_params=_cparams(("arbitrary",)),
        name="moe_combine_ln3",
    )(pos, pos, y_slots, x2, r, ln)


def _route_slots(e1, e2, tb, tm):
    n = e1.shape[0]
    flat_e = jnp.stack([e1, e2], axis=1).reshape(-1)
    n_assign = 2 * n
    n_blocks = -(-n_assign // tb) + N_EXPERTS
    cap = n_blocks * tb
    experts = jnp.arange(N_EXPERTS, dtype=jnp.int32)
    counts = jnp.sum((flat_e[:, None] == experts[None, :]).astype(jnp.int32), axis=0)
    padded = (counts + tb - 1) // tb * tb
    pad_end = jnp.cumsum(padded)
    need_end = jnp.cumsum(padded - counts)
    fill = jnp.arange(cap - n_assign, dtype=jnp.int32)
    fill_e = jnp.sum((need_end[None, :] <= fill[:, None]).astype(jnp.int32), axis=1)
    keys = jnp.concatenate([flat_e, fill_e])
    a_id = jnp.concatenate([jnp.arange(n_assign, dtype=jnp.int32),
                            n_assign + jnp.arange(cap - n_assign, dtype=jnp.int32)])
    _, slot_a = lax.sort((keys, a_id), num_keys=1, is_stable=True)
    slot_tok = jnp.where(slot_a < n_assign, slot_a // 2, 0).astype(jnp.int32)
    _, slot_of = lax.sort((slot_a, jnp.arange(cap, dtype=jnp.int32)), num_keys=1)
    slot_of = slot_of[:n_assign]
    starts = jnp.arange(n_blocks, dtype=jnp.int32) * tb
    block_e = jnp.minimum(jnp.sum((pad_end[None, :] <= starts[:, None]).astype(jnp.int32), axis=1),
                          N_EXPERTS - 1).astype(jnp.int32)
    block_nv = jnp.sum((slot_a.reshape(n_blocks, tb) < n_assign).astype(jnp.int32), axis=1)
    pos = slot_of.reshape(n // tm, tm, 2).transpose(0, 2, 1).reshape(n // tm, 1, 2 * tm)
    return block_e, block_nv, slot_tok, pos


def _pad_rows(a, rows=SUBLANES):
    return jnp.concatenate([a, jnp.zeros((rows - a.shape[0],) + a.shape[1:], a.dtype)], axis=0)


def _layer(x, mem, w_in, hgrn_lb, hgrn_norm_w, gdn_conv_w, gdn_a_log, gdn_dt_bias, gdn_norm_w, w_out,
           ln1_g, ln1_b, xa_w_q, xa_w_kv, xa_w_o, ln2_g, ln2_b, moe_w_group, moe_b_group,
           moe_w_expert, moe_b_expert, moe_w_gate, moe_w_up, moe_w_down, ln3_g, ln3_b,
           *, nb_first=1, tm_mm=1024, tbk=512, tm_row=256):
    nb, t, _ = x.shape
    n = nb * t
    xf = x.reshape(n, D_MODEL)
    layer = 0

    w_in_b = w_in[layer].astype(BF16)
    n_main = 9 * SEG
    w_small = jnp.concatenate([w_in_b[:, n_main:], jnp.zeros((D_MODEL, LANES - 4 * N_HEADS), BF16)], axis=1)
    xb = xf.astype(BF16)
    proj = _matmul(xb, w_in_b[:, :n_main], tm_mm, SEG, F32)
    ps = _matmul(xb, w_small, tm_mm, LANES, F32)

    lb2 = jnp.cumsum(jax.nn.softmax(hgrn_lb.astype(F32), axis=0), axis=0)[layer]
    oh = _hgrn2(proj, lb2, nb, t, HG_TBK, HG_HPG)

    qkv = _gdn_prep(proj, gdn_conv_w[layer, :, 0, :].astype(F32), nb, t, tbk)
    coef = jnp.stack([-jnp.exp(gdn_a_log[layer].astype(F32)).reshape(-1),
                      gdn_dt_bias[layer].astype(F32).reshape(-1)], axis=0)
    og = _gdn(qkv, ps, ps.T, coef, nb, t, GD_TBK, GD_HPG)

    nw = _pad_rows(jnp.stack([hgrn_norm_w[layer], gdn_norm_w[layer]], axis=0).astype(F32))
    ln1 = _pad_rows(jnp.stack([ln1_g[layer], ln1_b[layer]], axis=0).astype(F32))
    x1 = _mixout(oh, og, proj, xf, w_out[layer].astype(BF16), nw, ln1, tm_row)

    n_mem = mem.shape[1]
    kv = _matmul(mem.reshape(nb * n_mem, D_MODEL).astype(BF16), xa_w_kv[layer].astype(BF16),
                 n_mem, 1024, BF16)
    ln2 = _pad_rows(jnp.stack([ln2_g[layer], ln2_b[layer]], axis=0).astype(F32))
    w_r = jnp.concatenate([moe_w_expert[layer], moe_w_group[layer],
                           jnp.zeros((D_MODEL, LANES - N_EXPERTS - N_GROUPS), F32)], axis=1).astype(F32)
    w_r_hi = w_r.astype(BF16)
    w_r_mid = (w_r - w_r_hi.astype(F32)).astype(BF16)
    wr = jnp.concatenate([w_r_hi, w_r_mid], axis=1)
    b_r = jnp.concatenate([moe_b_expert[layer], moe_b_group[layer],
                           jnp.zeros((LANES - N_EXPERTS - N_GROUPS,), F32)]).astype(F32)
    br = jnp.broadcast_to(b_r[None, :], (SUBLANES, LANES))
    x2, r = _attn(x1, kv, xa_w_q[layer].astype(BF16), xa_w_o[layer].astype(BF16), ln2, wr, br, nb, t, tm_row)

    e1 = r[:, 0].astype(jnp.int32)
    e2 = r[:, 1].astype(jnp.int32)
    block_e, block_nv, slot_tok, pos = _route_slots(e1, e2, MOE_TB, tm_row)
    y_slots = _moe(x2, block_e, block_nv, slot_tok, moe_w_gate[layer].astype(BF16),
                   moe_w_up[layer].astype(BF16), moe_w_down[layer].astype(BF16))

    ln3 = _pad_rows(jnp.stack([ln3_g[layer], ln3_b[layer]], axis=0).astype(F32))
    y_first, y_rest = _combine(y_slots, pos, x2, r, ln3, tm_row, nb_first * t)
    return y_first.reshape(nb_first, t, D_MODEL), y_rest.reshape(nb - nb_first, t, D_MODEL)


def kernel(x_prompt, x_sample, mem_prompt, mem_sample, w_in, hgrn_lb, hgrn_norm_w, gdn_conv_w, gdn_a_log, gdn_dt_bias, gdn_norm_w, w_out, ln1_g, ln1_b, xa_w_q, xa_w_kv, xa_w_o, ln2_g, ln2_b, moe_w_group, moe_b_group, moe_w_expert, moe_b_expert, moe_w_gate, moe_w_up, moe_w_down, ln3_g, ln3_b):
    assert x_prompt.shape[1] == x_sample.shape[1]
    bp = x_prompt.shape[0]
    x = jnp.concatenate([x_prompt, x_sample], axis=0)
    mem = jnp.concatenate([mem_prompt, mem_sample], axis=0)
    return _layer(x, mem, w_in, hgrn_lb, hgrn_norm_w, gdn_conv_w, gdn_a_log, gdn_dt_bias, gdn_norm_w, w_out,
                  ln1_g, ln1_b, xa_w_q, xa_w_kv, xa_w_o, ln2_g, ln2_b, moe_w_group, moe_b_group,
                  moe_w_expert, moe_b_expert, moe_w_gate, moe_w_up, moe_w_down, ln3_g, ln3_b, nb_first=bp)
```

```python
import functools
import math

import numpy as np
import jax
import jax.numpy as jnp
from jax import lax
from jax.experimental import pallas as pl
from jax.experimental.pallas import tpu as pltpu

F32 = jnp.float32
BF16 = jnp.bfloat16

D_MODEL = 2048
N_HEADS = 8
D_HEAD = 128
SEG = N_HEADS * D_HEAD
CONV_K = 5
XA_HEADS = 4
XA_HEAD_DIM = D_MODEL // XA_HEADS
N_GROUPS = 4
EXPERTS_PER_GROUP = 8
N_EXPERTS = N_GROUPS * EXPERTS_PER_GROUP
D_EXPERT = 512
DN_ALPHA = 2.0 ** 0.25
LN_EPS = 1e-5
NORM_EPS = 1e-6

LANES = 128
SUBLANES = 8
VMEM_LIMIT = 56 * 1024 * 1024

HG_CHUNK = 128
HG_TBK = 256
HG_HPG = 8
HG_SMALL_LEVELS = 3
GD_CHUNK = 128
GD_TBK = 256
GD_HPG = 8
MOE_TB = 256


def _cparams(sem):
    return pltpu.CompilerParams(dimension_semantics=sem, vmem_limit_bytes=VMEM_LIMIT)


def _dot(a, b):
    return jnp.dot(a, b, preferred_element_type=F32)


def _dot_nt(a, b):
    return lax.dot_general(a, b, (((1,), (1,)), ((), ())), preferred_element_type=F32)


def _dot_tn(a, b):
    return lax.dot_general(a, b, (((0,), (0,)), ((), ())), preferred_element_type=F32)


def _split2(x):
    hi = x.astype(BF16)
    mid = (x - hi.astype(F32)).astype(BF16)
    return hi, mid


def _sigmoid(x):
    return 1.0 / (1.0 + jnp.exp(-x))


def _layernorm_rows(y, g, b):
    mu = jnp.mean(y, axis=-1, keepdims=True)
    yc = y - mu
    var = jnp.mean(yc * yc, axis=-1, keepdims=True)
    return yc * lax.rsqrt(var + LN_EPS) * g + b


def _mm_kernel(x_ref, w_ref, o_ref):
    o_ref[...] = _dot(x_ref[...], w_ref[...]).astype(o_ref.dtype)


def _matmul(x, w, tm, tn, out_dtype):
    m, k = x.shape
    n = w.shape[1]
    return pl.pallas_call(
        _mm_kernel,
        grid=(m // tm, n // tn),
        in_specs=[pl.BlockSpec((tm, k), lambda i, j: (i, 0)),
                  pl.BlockSpec((k, tn), lambda i, j: (0, j))],
        out_specs=pl.BlockSpec((tm, tn), lambda i, j: (i, j)),
        out_shape=jax.ShapeDtypeStruct((m, n), out_dtype),
        compiler_params=_cparams(("parallel", "arbitrary")),
        name="dense_matmul",
    )(x, w)


def _mm2_kernel(xa_ref, xb_ref, w_ref, o_ref, *, na):
    i = pl.program_id(0)

    @pl.when(i < na)
    def _():
        o_ref[...] = _dot(xa_ref[...].astype(BF16), w_ref[...]).astype(o_ref.dtype)

    @pl.when(i >= na)
    def _():
        o_ref[...] = _dot(xb_ref[...].astype(BF16), w_ref[...]).astype(o_ref.dtype)


def _matmul_rows2(xa, xb, w, tm, tn, out_dtype):
    k = xa.shape[1]
    n = w.shape[1]
    na = xa.shape[0] // tm
    nt = na + xb.shape[0] // tm
    return pl.pallas_call(
        functools.partial(_mm2_kernel, na=na),
        grid=(nt, n // tn),
        in_specs=[pl.BlockSpec((tm, k), lambda i, j: (jnp.minimum(i, na - 1), 0)),
                  pl.BlockSpec((tm, k), lambda i, j: (jnp.maximum(i - na, 0), 0)),
                  pl.BlockSpec((k, tn), lambda i, j: (0, j))],
        out_specs=pl.BlockSpec((tm, tn), lambda i, j: (i, j)),
        out_shape=jax.ShapeDtypeStruct((nt * tm, n), out_dtype),
        compiler_params=_cparams(("parallel", "arbitrary")),
        name="in_proj_matmul",
    )(xa, xb, w)


def _hgrn_constants(c):
    nl = int(math.log2(c))
    t = np.arange(c)
    u = np.arange(c)
    blocks = []
    blocks.append((u[None, :] <= t[:, None]).astype(np.float32))
    masks, qrows = [], []
    for l in range(nl):
        cc = 1 << l
        m = (t // (2 * cc)) * (2 * cc) + cc
        upper = ((t >> l) & 1) == 1
        a = np.zeros((c, c), np.float32)
        for r in range(c):
            if upper[r]:
                a[r, m[r]:r + 1] = 1.0
            else:
                a[r, r + 1:m[r]] = 1.0
        if l < HG_SMALL_LEVELS:
            blocks.append(a)
        same = (t[:, None] >> (l + 1)) == (t[None, :] >> (l + 1))
        masks.append((same & upper[:, None] & (~upper)[None, :]).astype(np.float32))
        qrows.append(np.broadcast_to(upper[:, None], (c, D_HEAD)).astype(np.float32))
    blocks.append(np.ones((16, c), np.float32))
    a_f = np.concatenate(blocks, axis=0)
    m_f = np.stack(masks)
    q_f = np.stack(qrows)
    a_b = np.concatenate([b[::-1, ::-1] for b in blocks], axis=0)
    m_b = m_f[:, ::-1, ::-1]
    q_b = q_f[:, ::-1, :]
    return (np.stack([a_f, a_b]), np.stack([m_f, m_b]).copy(), np.stack([q_f, q_b]).copy(), nl)


def _hgrn_kernel(q_ref, f_ref, v_ref, lb_ref, a_ref, m_ref, qr_ref, o_ref, st_ref, *, c, nck, nl, scale, hpg):
    d = pl.program_id(0)
    i = pl.program_id(3)

    @pl.when(i == 0)
    def _():
        st_ref[...] = jnp.zeros_like(st_ref)

    probs = [(j, g) for j in range(nck) for g in range(hpg)]

    def rows(j):
        jj = jnp.where(d == 0, j, nck - 1 - j)
        return pl.ds(pl.multiple_of(jj * c, c), c)

    def cols(g):
        return slice(g * D_HEAD, (g + 1) * D_HEAD)

    def eblk(k, g2, rows=c):
        e2 = _dot(a_ref[pl.ds(k * c, rows), :], g2)
        return e2[:, :D_HEAD] + e2[:, D_HEAD:]

    def boundary(gcum, l):
        cc = 1 << l
        pieces = []
        for p in range(c // (2 * cc)):
            m = p * 2 * cc + cc
            rowv = jnp.where(d == 0, gcum[m - 1:m, :], gcum[m:m + 1, :])
            pieces.append(jnp.broadcast_to(rowv, (2 * cc, D_HEAD)))
        return pieces[0] if len(pieces) == 1 else jnp.concatenate(pieces, axis=0)

    qs, kk, vb, vv, g2, gcum, gtot = {}, {}, {}, {}, {}, {}, {}
    for (j, g) in probs:
        lb = lb_ref[g, 0:1, :]
        one_m_lb = 1.0 - lb
        fr = f_ref[rows(j), cols(g)]
        qs[j, g] = q_ref[rows(j), cols(g)] * scale
        v = v_ref[rows(j), cols(g)]
        vv[j, g] = v
        vb[j, g] = v.astype(BF16)
        gl = jnp.log(lb + one_m_lb * _sigmoid(fr))
        kk[j, g] = one_m_lb * _sigmoid(-fr)
        g_hi, g_mid = _split2(gl)
        g2[j, g] = jnp.concatenate([g_hi, g_mid], axis=1)
    for p in probs:
        gcum[p] = eblk(0, g2[p])
        gtot[p] = eblk(1 + HG_SMALL_LEVELS, g2[p], rows=16)[0:1, :]

    s = {}
    for l in range(nl):
        if l < HG_SMALL_LEVELS:
            e = {p: eblk(1 + l, g2[p]) for p in probs}
        else:
            e = {p: -jnp.abs(gcum[p] - boundary(gcum[p], l)) for p in probs}
        w = {p: (jnp.where(qr_ref[l] > 0.5, qs[p], kk[p]) * jnp.exp(e[p])).astype(BF16) for p in probs}
        sd = {p: _dot_nt(w[p], w[p]) for p in probs}
        for p in probs:
            s[p] = m_ref[l] * sd[p] if l == 0 else s[p] + m_ref[l] * sd[p]

    o_in, qd, upd, dec = {}, {}, {}, {}
    for p in probs:
        dg = jnp.sum(qs[p] * kk[p], axis=-1, keepdims=True)
        o_in[p] = _dot(s[p].astype(BF16), vb[p]) + dg * vv[p]
        qd[p] = (qs[p] * jnp.exp(gcum[p])).astype(BF16)
        kd = (kk[p] * jnp.exp(gtot[p] - gcum[p])).astype(BF16)
        upd[p] = _dot_tn(vb[p], kd)
        dec[p] = jnp.exp(gtot[p])

    st = [st_ref[g] for g in range(hpg)]
    for j in range(nck):
        for g in range(hpg):
            p = (j, g)
            o_ref[rows(j), cols(g)] = o_in[p] + _dot_nt(qd[p], st[g].astype(BF16))
            st[g] = st[g] * dec[p] + upd[p]
    for g in range(hpg):
        st_ref[g] = st[g]


def _hgrn2(proj, lb2, nb, t, tbk, hpg):
    c = HG_CHUNK
    n = nb * t
    nblk = t // tbk
    a_np, m_np, q_np, nl = _hgrn_constants(c)
    a_all = jnp.asarray(a_np, BF16)
    masks = jnp.asarray(m_np, F32)
    qrows = jnp.asarray(q_np, F32)
    lb8 = jnp.broadcast_to(lb2.reshape(2, N_HEADS, 1, D_HEAD), (2, N_HEADS, SUBLANES, D_HEAD))
    ngrp = N_HEADS // hpg
    wb = hpg * D_HEAD

    def row(d, b, i):
        return b * nblk + jnp.where(d == 0, i, nblk - 1 - i)

    kern = functools.partial(_hgrn_kernel, c=c, nck=tbk // c, nl=nl, scale=D_HEAD ** -0.5, hpg=hpg)
    na = a_np.shape[1]
    return pl.pallas_call(
        kern,
        grid=(2, nb, ngrp, nblk),
        in_specs=[
            pl.BlockSpec((tbk, wb), lambda d, b, h, i: (row(d, b, i), h)),
            pl.BlockSpec((tbk, wb), lambda d, b, h, i: (row(d, b, i), (1 + d) * ngrp + h)),
            pl.BlockSpec((tbk, wb), lambda d, b, h, i: (row(d, b, i), 3 * ngrp + h)),
            pl.BlockSpec((None, hpg, SUBLANES, D_HEAD), lambda d, b, h, i: (d, h, 0, 0)),
            pl.BlockSpec((None, na, c), lambda d, b, h, i: (d, 0, 0)),
            pl.BlockSpec((None, nl, c, c), lambda d, b, h, i: (d, 0, 0, 0)),
            pl.BlockSpec((None, nl, c, D_HEAD), lambda d, b, h, i: (d, 0, 0, 0)),
        ],
        out_specs=pl.BlockSpec((None, tbk, wb), lambda d, b, h, i: (d, row(d, b, i), h)),
        out_shape=jax.ShapeDtypeStruct((2, n, SEG), F32),
        scratch_shapes=[pltpu.VMEM((hpg, D_HEAD, D_HEAD), F32)],
        compiler_params=_cparams(("parallel", "parallel", "parallel", "arbitrary")),
        name="hgrn2_bidir",
    )(proj, proj, proj, lb8, a_all, masks, qrows)


def _gdn_prep_kernel(prev_ref, cur_ref, nxt_ref, w_ref, o_ref, *, tbk, scale):
    i = pl.program_id(1)
    s = pl.program_id(2)
    nblk = pl.num_programs(1)
    prev = jnp.where(i == 0, 0.0, prev_ref[...])
    nxt = jnp.where(i == nblk - 1, 0.0, nxt_ref[...])
    full = jnp.concatenate([prev, cur_ref[...], nxt], axis=0)
    rows = tbk + 2 * SUBLANES
    acc = jnp.zeros((tbk, SEG), F32)
    for j in range(CONV_K):
        shift = (CONV_K // 2 - j) % rows
        sh = full if shift == 0 else pltpu.roll(full, shift, axis=0)
        acc = acc + sh[SUBLANES:SUBLANES + tbk, :] * w_ref[j:j + 1, :]
    y = acc * _sigmoid(acc)
    fac_all = jnp.where(s == 0, scale, 1.0)
    for h in range(N_HEADS):
        ys = y[:, h * D_HEAD:(h + 1) * D_HEAD]
        ss = jnp.sum(ys * ys, axis=-1, keepdims=True)
        fac = jnp.where(s < 2, lax.rsqrt(ss + NORM_EPS), 1.0) * fac_all
        o_ref[:, h * D_HEAD:(h + 1) * D_HEAD] = ys * fac


def _gdn_prep(proj, conv_w, nb, t, tbk):
    n = nb * t
    nblk = t // tbk
    hb = tbk // SUBLANES
    w8 = jnp.concatenate([conv_w, jnp.zeros((SUBLANES - CONV_K, 3 * SEG), F32)], axis=0)
    last = n // SUBLANES - 1
    kern = functools.partial(_gdn_prep_kernel, tbk=tbk, scale=D_HEAD ** -0.5)
    return pl.pallas_call(
        kern,
        grid=(nb, nblk, 3),
        in_specs=[
            pl.BlockSpec((SUBLANES, SEG), lambda b, i, s: (jnp.maximum((b * nblk + i) * hb - 1, 0), 5 + s)),
            pl.BlockSpec((tbk, SEG), lambda b, i, s: (b * nblk + i, 5 + s)),
            pl.BlockSpec((SUBLANES, SEG), lambda b, i, s: (jnp.minimum((b * nblk + i + 1) * hb, last), 5 + s)),
            pl.BlockSpec((SUBLANES, SEG), lambda b, i, s: (0, s)),
        ],
        out_specs=pl.BlockSpec((None, tbk, SEG), lambda b, i, s: (s, b * nblk + i, 0)),
        out_shape=jax.ShapeDtypeStruct((3, n, SEG), F32),
        compiler_params=_cparams(("parallel", "parallel", "arbitrary")),
        name="gdn_conv_prep",
    )(proj, proj, proj, w8)


def _gdn_constants(c):
    nl = int(math.log2(c))
    t = np.arange(c)
    incl = (t[None, :] <= t[:, None]).astype(np.float32)
    strict = (t[None, :] < t[:, None]).astype(np.float32)
    after = (t[:, None] > t[None, :]).astype(np.float32)
    rhs_f = np.concatenate([np.ones((c, LANES), np.float32), after], axis=1)
    lvl = []
    for l in range(nl):
        upper = ((t >> l) & 1) == 1
        same = (t[:, None] >> (l + 1)) == (t[None, :] >> (l + 1))
        lvl.append((same & upper[:, None] & (~upper)[None, :]).astype(np.float32))
    lvl = np.stack(lvl)
    mk_f = np.concatenate([incl[None], strict[None], lvl], axis=0)
    mk_b = mk_f[:, ::-1, ::-1]
    rhs_b = np.concatenate([np.ones((c, LANES), np.float32), after[::-1, ::-1]], axis=1)
    return np.stack([mk_f, mk_b]).copy(), np.stack([rhs_f, rhs_b]).copy(), nl


def _gdn_kernel(coef_ref, q_ref, k_ref, v_ref, arow_ref, ps_ref, mk_ref, rhs_ref, o_ref, st_ref, *, c, nck, nl, hpg):
    d = pl.program_id(0)
    hg = pl.program_id(2)
    i = pl.program_id(3)

    @pl.when(i == 0)
    def _():
        st_ref[...] = jnp.zeros_like(st_ref)

    lane = lax.broadcasted_iota(jnp.int32, (c, LANES), 1)
    incl = mk_ref[0]
    strict = mk_ref[1]
    eye = incl - strict
    rhs = rhs_ref[...]
    probs = [(j, g) for j in range(nck) for g in range(hpg)]

    def rows(j):
        jj = jnp.where(d == 0, j, nck - 1 - j)
        return pl.ds(pl.multiple_of(jj * c, c), c)

    def cols(g):
        return slice(g * D_HEAD, (g + 1) * D_HEAD)

    qn, kn, vv, beta, gcum, dec, etot = {}, {}, {}, {}, {}, {}, {}
    for (j, g) in probs:
        h = hg * hpg + g
        neg_ea = coef_ref[0, d * N_HEADS + h]
        dtb = coef_ref[1, d * N_HEADS + h]
        a_blk = arow_ref[pl.ds(h, 1), :]
        a_row = jnp.where(d == 0, a_blk[:, j * c:(j + 1) * c], a_blk[:, (nck - 1 - j) * c:(nck - j) * c])
        z = a_row + dtb
        g_row = neg_ea * (jnp.maximum(z, 0.0) + jnp.log(1.0 + jnp.exp(-jnp.abs(z))))
        beta_lane = 2 * N_HEADS + d * N_HEADS + h
        b_col = jnp.sum(jnp.where(lane == beta_lane, ps_ref[rows(j), :], 0.0), axis=-1, keepdims=True)
        beta[j, g] = _sigmoid(b_col)
        lg = incl * g_row
        lg_hi, lg_mid = _split2(lg)
        out = _dot(jnp.concatenate([lg_hi, lg_mid], axis=0), rhs)
        out = out[:c] + out[c:]
        gcum[j, g] = out[:, :LANES]
        dmat = out[:, LANES:]
        dec[j, g] = jnp.exp(jnp.minimum(dmat, 0.0)) * incl
        qn[j, g] = q_ref[rows(j), cols(g)]
        kn[j, g] = k_ref[rows(j), cols(g)]
        vv[j, g] = v_ref[rows(j), cols(g)]

    kb, knb, a_dec, tinv = {}, {}, {}, {}
    for p in probs:
        kb[p] = kn[p] * beta[p]
        knb[p] = kn[p].astype(BF16)
        a_dec[p] = _dot_nt(kb[p].astype(BF16), knb[p]) * dec[p]
        tinv[p] = eye - a_dec[p] * mk_ref[2]
    for l in range(1, nl):
        ml = mk_ref[2 + l]
        tb = {p: tinv[p].astype(BF16) for p in probs}
        ta = {p: _dot(tb[p], (a_dec[p] * ml).astype(BF16)).astype(BF16) for p in probs}
        for p in probs:
            tinv[p] = tinv[p] - _dot(ta[p], tb[p])

    u, w, qk, qe, ke = {}, {}, {}, {}, {}
    for p in probs:
        gtot = jnp.where(d == 0, gcum[p][c - 1:c, :], gcum[p][0:1, :])
        eg = jnp.exp(gcum[p])
        etot[p] = jnp.exp(gtot)
        uw = _dot(tinv[p].astype(BF16), jnp.concatenate([vv[p] * beta[p], kb[p] * eg], axis=1).astype(BF16))
        u[p] = uw[:, :D_HEAD]
        w[p] = uw[:, D_HEAD:].astype(BF16)
        qk[p] = (_dot_nt(qn[p].astype(BF16), knb[p]) * dec[p]).astype(BF16)
        qe[p] = (qn[p] * eg).astype(BF16)
        ke[p] = (kn[p] * jnp.exp(gtot - gcum[p])).astype(BF16)

    st = [st_ref[g] for g in range(hpg)]
    for j in range(nck):
        stb = [st[g].astype(BF16) for g in range(hpg)]
        ws = [_dot(w[j, g], stb[g]) for g in range(hpg)]
        qs = [_dot(qe[j, g], stb[g]) for g in range(hpg)]
        vnb = [(u[j, g] - ws[g]).astype(BF16) for g in range(hpg)]
        for g in range(hpg):
            o_ref[rows(j), cols(g)] = qs[g] + _dot(qk[j, g], vnb[g])
        for g in range(hpg):
            st[g] = st[g] * etot[j, g] + _dot_tn(ke[j, g], vnb[g])
    for g in range(hpg):
        st_ref[g] = st[g]


def _gdn(qkv, ps, ps_t, coef, nb, t, tbk, hpg):
    c = GD_CHUNK
    n = nb * t
    nblk = t // tbk
    mk_np, rhs_np, nl = _gdn_constants(c)
    mk = jnp.asarray(mk_np, F32)
    rhs = jnp.asarray(rhs_np, BF16)
    wb = hpg * D_HEAD

    def row(d, b, i):
        return b * nblk + jnp.where(d == 0, i, nblk - 1 - i)

    kern = functools.partial(_gdn_kernel, c=c, nck=tbk // c, nl=nl, hpg=hpg)
    grid_spec = pltpu.PrefetchScalarGridSpec(
        num_scalar_prefetch=0,
        grid=(2, nb, N_HEADS // hpg, nblk),
        in_specs=[
            pl.BlockSpec(memory_space=pltpu.SMEM),
            pl.BlockSpec((None, tbk, wb), lambda d, b, h, i: (0, row(d, b, i), h)),
            pl.BlockSpec((None, tbk, wb), lambda d, b, h, i: (1, row(d, b, i), h)),
            pl.BlockSpec((None, tbk, wb), lambda d, b, h, i: (2, row(d, b, i), h)),
            pl.BlockSpec((SUBLANES, tbk), lambda d, b, h, i: (d, row(d, b, i))),
            pl.BlockSpec((tbk, LANES), lambda d, b, h, i: (row(d, b, i), 0)),
            pl.BlockSpec((None, 2 + nl, c, c), lambda d, b, h, i: (d, 0, 0, 0)),
            pl.BlockSpec((None, c, c + LANES), lambda d, b, h, i: (d, 0, 0)),
        ],
        out_specs=pl.BlockSpec((None, tbk, wb), lambda d, b, h, i: (d, row(d, b, i), h)),
        scratch_shapes=[pltpu.VMEM((hpg, D_HEAD, D_HEAD), F32)],
    )
    return pl.pallas_call(
        kern,
        grid_spec=grid_spec,
        out_shape=jax.ShapeDtypeStruct((2, n, SEG), F32),
        compiler_params=_cparams(("parallel", "parallel", "parallel", "arbitrary")),
        name="gdn_bidir",
    )(coef, qkv, qkv, qkv, ps_t, ps, mk, rhs)


def _mixout_kernel(oh_ref, og_ref, hg_ref, gz_ref, xa_ref, xb_ref, w_ref, nw_ref, ln_ref, o_ref, lhs_ref, *, na):
    i = pl.program_id(0)

    def gated(o2_ref, gate_ref, wrow, col0):
        for h in range(N_HEADS):
            sl = slice(h * D_HEAD, (h + 1) * D_HEAD)
            o = o2_ref[0, :, sl] + o2_ref[1, :, sl]
            ms = jnp.mean(o * o, axis=-1, keepdims=True)
            gt = gate_ref[:, sl]
            y = o * lax.rsqrt(ms + NORM_EPS) * wrow * (gt * _sigmoid(gt))
            lhs_ref[:, col0 + h * D_HEAD:col0 + (h + 1) * D_HEAD] = y.astype(BF16)

    gated(oh_ref, hg_ref, nw_ref[0:1, :], 0)
    gated(og_ref, gz_ref, nw_ref[1:2, :], SEG)
    hmix = _dot(lhs_ref[...], w_ref[...])
    x = jnp.where(i < na, xa_ref[...], xb_ref[...])
    y = DN_ALPHA * x + hmix
    o_ref[...] = _layernorm_rows(y, ln_ref[0:1, :], ln_ref[1:2, :])


def _mixout(oh, og, proj, xa, xb, w_out_b, nw, ln, tm):
    na = xa.shape[0] // tm
    n = xa.shape[0] + xb.shape[0]
    return pl.pallas_call(
        functools.partial(_mixout_kernel, na=na),
        grid=(n // tm,),
        in_specs=[
            pl.BlockSpec((2, tm, SEG), lambda i: (0, i, 0)),
            pl.BlockSpec((2, tm, SEG), lambda i: (0, i, 0)),
            pl.BlockSpec((tm, SEG), lambda i: (i, 4)),
            pl.BlockSpec((tm, SEG), lambda i: (i, 8)),
            pl.BlockSpec((tm, D_MODEL), lambda i: (jnp.minimum(i, na - 1), 0)),
            pl.BlockSpec((tm, D_MODEL), lambda i: (jnp.maximum(i - na, 0), 0)),
            pl.BlockSpec((2 * SEG, D_MODEL), lambda i: (0, 0)),
            pl.BlockSpec((SUBLANES, D_HEAD), lambda i: (0, 0)),
            pl.BlockSpec((SUBLANES, D_MODEL), lambda i: (0, 0)),
        ],
        out_specs=pl.BlockSpec((tm, D_MODEL), lambda i: (i, 0)),
        out_shape=jax.ShapeDtypeStruct((n, D_MODEL), F32),
        scratch_shapes=[pltpu.VMEM((tm, 2 * SEG), BF16)],
        compiler_params=_cparams(("parallel",)),
        name="mixer_out_ln1",
    )(oh, og, proj, proj, xa, xb, w_out_b, nw, ln)


def _attn_kernel(x_ref, kv_ref, wq_ref, wo_ref, ln_ref, wr_ref, br_ref, o_ref, r_ref, att_ref):
    x1 = x_ref[...]
    q = _dot(x1.astype(BF16), wq_ref[...])
    sc = XA_HEAD_DIM ** -0.5
    for h in range(XA_HEADS):
        sl = slice(h * XA_HEAD_DIM, (h + 1) * XA_HEAD_DIM)
        kh = kv_ref[:, sl]
        vh = kv_ref[:, D_MODEL + h * XA_HEAD_DIM:D_MODEL + (h + 1) * XA_HEAD_DIM]
        s = _dot_nt(q[:, sl].astype(BF16), kh) * sc
        s = s - jnp.max(s, axis=-1, keepdims=True)
        p = jnp.exp(s)
        p = p / jnp.sum(p, axis=-1, keepdims=True)
        att_ref[:, sl] = _dot(p.astype(BF16), vh).astype(BF16)
    hx = _dot(att_ref[...], wo_ref[...])
    x2 = _layernorm_rows(DN_ALPHA * x1 + hx, ln_ref[0:1, :], ln_ref[1:2, :])
    o_ref[...] = x2

    x_hi, x_mid = _split2(x2)
    lg2 = _dot(x_hi, wr_ref[...])
    lg = lg2[:, :LANES] + lg2[:, LANES:] + _dot(x_mid, wr_ref[:, :LANES]) + br_ref[0:1, :]
    tm = lg.shape[0]
    lane_i = lax.broadcasted_iota(jnp.int32, (tm, LANES), 1)
    lane = lane_i.astype(F32)
    grp_of_lane = (lane_i >> 3).astype(F32)
    neg = jnp.float32(-1e30)
    big = jnp.float32(LANES)
    is_g = (lane_i >= N_EXPERTS) & (lane_i < N_EXPERTS + N_GROUPS)
    gl = jnp.where(is_g, lg, neg)
    gmax = jnp.max(gl, axis=-1, keepdims=True)
    gsel = jnp.min(jnp.where(gl == gmax, lane, big), axis=-1, keepdims=True) - N_EXPERTS
    p_group = 1.0 / jnp.sum(jnp.where(is_g, jnp.exp(gl - gmax), 0.0), axis=-1, keepdims=True)
    in_grp = (lane_i < N_EXPERTS) & (grp_of_lane == gsel)
    el = jnp.where(in_grp, lg, neg)
    m1 = jnp.max(el, axis=-1, keepdims=True)
    i1 = jnp.min(jnp.where(el == m1, lane, big), axis=-1, keepdims=True)
    el2 = jnp.where(lane == i1, neg, el)
    m2 = jnp.max(el2, axis=-1, keepdims=True)
    i2 = jnp.min(jnp.where(el2 == m2, lane, big), axis=-1, keepdims=True)
    e2 = jnp.exp(m2 - m1)
    g1 = p_group / (1.0 + e2)
    g2 = p_group * e2 / (1.0 + e2)
    r = jnp.where(lane_i == 0, i1, 0.0)
    r = jnp.where(lane_i == 1, i2, r)
    r = jnp.where(lane_i == 2, g1, r)
    r = jnp.where(lane_i == 3, g2, r)
    r_ref[...] = r


def _attn(x1, kv, wq_b, wo_b, ln, wr, br, nb, t, tm):
    n = nb * t
    nblk = t // tm
    n_mem = kv.shape[0] // nb
    return pl.pallas_call(
        _attn_kernel,
        grid=(nb, nblk),
        in_specs=[
            pl.BlockSpec((tm, D_MODEL), lambda b, i: (b * nblk + i, 0)),
            pl.BlockSpec((n_mem, 2 * D_MODEL), lambda b, i: (b, 0)),
            pl.BlockSpec((D_MODEL, D_MODEL), lambda b, i: (0, 0)),
            pl.BlockSpec((D_MODEL, D_MODEL), lambda b, i: (0, 0)),
            pl.BlockSpec((SUBLANES, D_MODEL), lambda b, i: (0, 0)),
            pl.BlockSpec((D_MODEL, 2 * LANES), lambda b, i: (0, 0)),
            pl.BlockSpec((SUBLANES, LANES), lambda b, i: (0, 0)),
        ],
        out_specs=[pl.BlockSpec((tm, D_MODEL), lambda b, i: (b * nblk + i, 0)),
                   pl.BlockSpec((tm, LANES), lambda b, i: (b * nblk + i, 0))],
        out_shape=[jax.ShapeDtypeStruct((n, D_MODEL), F32), jax.ShapeDtypeStruct((n, LANES), F32)],
        scratch_shapes=[pltpu.VMEM((tm, D_MODEL), BF16)],
        compiler_params=_cparams(("parallel", "arbitrary")),
        name="xattn_ln2_router",
    )(x1, kv, wq_b, wo_b, ln, wr, br)


DMA_UNROLL = 8


def _issue_row_gather(idx_ref, src_hbm, dst, sem, n_rows):
    def body(k, carry):
        for u in range(DMA_UNROLL):
            r = k * DMA_UNROLL + u
            pltpu.make_async_copy(src_hbm.at[pl.ds(idx_ref[0, r], 1)], dst.at[pl.ds(r, 1)], sem).start(priority=u % 2)
        return carry

    lax.fori_loop(0, n_rows // DMA_UNROLL, body, 0)


def _wait_row_gather(src_hbm, dst, sem, n_rows):
    pltpu.make_async_copy(src_hbm.at[pl.ds(0, n_rows)], dst, sem).wait()


def _moe_kernel(be_ref, nv_ref, tok_ref, tokn_ref, x_hbm, wg_ref, wu_ref, wd_ref, o_ref, xbuf, sem,
                wgb, wub, wdb, *, tb):
    b = pl.program_id(0)
    nb = pl.num_programs(0)
    slot = b % 2

    @pl.when((b == 0) | (be_ref[b] != be_ref[jnp.maximum(b - 1, 0)]))
    def _():
        wgb[...] = wg_ref[...].astype(BF16)
        wub[...] = wu_ref[...].astype(BF16)
        wdb[...] = wd_ref[...].astype(BF16)

    @pl.when((b == 0) & (nv_ref[0] > 0))
    def _():
        _issue_row_gather(tok_ref, x_hbm, xbuf.at[0], sem.at[0], tb)

    @pl.when((b + 1 < nb) & (nv_ref[jnp.minimum(b + 1, nb - 1)] > 0))
    def _():
        _issue_row_gather(tokn_ref, x_hbm, xbuf.at[1 - slot], sem.at[1 - slot], tb)

    @pl.when(nv_ref[b] > 0)
    def _():
        _wait_row_gather(x_hbm, xbuf.at[slot], sem.at[slot], tb)
        xb = xbuf[slot].astype(BF16)
        h1 = _dot(xb, wgb[...])
        h2 = _dot(xb, wub[...])
        hdn = (h1 * _sigmoid(h1) * h2).astype(BF16)
        o_ref[...] = _dot(hdn, wdb[...])

    @pl.when(nv_ref[b] == 0)
    def _():
        o_ref[...] = jnp.zeros_like(o_ref)


def _moe(x2, block_e, block_nv, slot_tok, wg_b, wu_b, wd_b):
    tb = MOE_TB
    cap = slot_tok.shape[0]
    n_blocks = cap // tb
    tok3 = slot_tok.reshape(n_blocks, 1, tb)
    grid_spec = pltpu.PrefetchScalarGridSpec(
        num_scalar_prefetch=2,
        grid=(n_blocks,),
        in_specs=[
            pl.BlockSpec((None, 1, tb), lambda b, be, nv: (b, 0, 0), memory_space=pltpu.SMEM),
            pl.BlockSpec((None, 1, tb), lambda b, be, nv: (jnp.minimum(b + 1, n_blocks - 1), 0, 0),
                         memory_space=pltpu.SMEM),
            pl.BlockSpec(memory_space=pl.ANY),
            pl.BlockSpec((None, D_MODEL, D_EXPERT), lambda b, be, nv: (be[b], 0, 0)),
            pl.BlockSpec((None, D_MODEL, D_EXPERT), lambda b, be, nv: (be[b], 0, 0)),
            pl.BlockSpec((None, D_EXPERT, D_MODEL), lambda b, be, nv: (be[b], 0, 0)),
        ],
        out_specs=pl.BlockSpec((tb, D_MODEL), lambda b, be, nv: (b, 0)),
        scratch_shapes=[pltpu.VMEM((2, tb, D_MODEL), F32), pltpu.SemaphoreType.DMA((2,)),
                        pltpu.VMEM((D_MODEL, D_EXPERT), BF16), pltpu.VMEM((D_MODEL, D_EXPERT), BF16),
                        pltpu.VMEM((D_EXPERT, D_MODEL), BF16)],
    )
    return pl.pallas_call(
        functools.partial(_moe_kernel, tb=tb),
        grid_spec=grid_spec,
        out_shape=jax.ShapeDtypeStruct((cap, D_MODEL), F32),
        compiler_params=_cparams(("arbitrary",)),
        name="moe_grouped_mlp",
    )(block_e, block_nv, tok3, tok3, x2, wg_b, wu_b, wd_b)


def _combine_kernel(pos_ref, posn_ref, y_hbm, x_ref, r_ref, ln_ref, op_ref, os_ref, ybuf, sem, *, tm, npt):
    i = pl.program_id(0)
    nt = pl.num_programs(0)
    slot = i % 2

    @pl.when(i == 0)
    def _():
        _issue_row_gather(pos_ref, y_hbm, ybuf.at[0], sem.at[0], 2 * tm)

    @pl.when(i + 1 < nt)
    def _():
        _issue_row_gather(posn_ref, y_hbm, ybuf.at[1 - slot], sem.at[1 - slot], 2 * tm)

    _wait_row_gather(y_hbm, ybuf.at[slot], sem.at[slot], 2 * tm)
    r = r_ref[...]
    y = r[:, 2:3] * ybuf[slot, 0:tm, :] + r[:, 3:4] * ybuf[slot, tm:2 * tm, :]
    res = _layernorm_rows(DN_ALPHA * x_ref[...] + y, ln_ref[0:1, :], ln_ref[1:2, :])

    @pl.when(i < npt)
    def _():
        op_ref[...] = res

    @pl.when(i >= npt)
    def _():
        os_ref[...] = res


def _combine(y_slots, pos, x2, r, ln, tm, n_first):
    n = x2.shape[0]
    nt = n // tm
    npt = n_first // tm
    grid_spec = pltpu.PrefetchScalarGridSpec(
        num_scalar_prefetch=0,
        grid=(nt,),
        in_specs=[
            pl.BlockSpec((None, 1, 2 * tm), lambda i: (i, 0, 0), memory_space=pltpu.SMEM),
            pl.BlockSpec((None, 1, 2 * tm), lambda i: (jnp.minimum(i + 1, nt - 1), 0, 0), memory_space=pltpu.SMEM),
            pl.BlockSpec(memory_space=pl.ANY),
            pl.BlockSpec((tm, D_MODEL), lambda i: (i, 0)),
            pl.BlockSpec((tm, LANES), lambda i: (i, 0)),
            pl.BlockSpec((SUBLANES, D_MODEL), lambda i: (0, 0)),
        ],
        out_specs=[pl.BlockSpec((tm, D_MODEL), lambda i: (jnp.minimum(i, npt - 1), 0)),
                   pl.BlockSpec((tm, D_MODEL), lambda i: (jnp.maximum(i - npt, 0), 0))],
        scratch_shapes=[pltpu.VMEM((2, 2 * tm, D_MODEL), F32), pltpu.SemaphoreType.DMA((2,))],
    )
    return pl.pallas_call(
        functools.partial(_combine_kernel, tm=tm, npt=npt),
        grid_spec=grid_spec,
        out_shape=[jax.ShapeDtypeStruct((n_first, D_MODEL), F32), jax.ShapeDtypeStruct((n - n_first, D_MODEL), F32)],
        compiler_params=_cparams(("arbitrary",)),
        name="moe_combine_ln3",
    )(pos, pos, y_slots, x2, r, ln)


def _route_slots(e1, e2, tb, tm):
    n = e1.shape[0]
    flat_e = jnp.stack([e1, e2], axis=1).reshape(-1)
    n_assign = 2 * n
    n_blocks = -(-n_assign // tb) + N_EXPERTS
    cap = n_blocks * tb
    experts = jnp.arange(N_EXPERTS, dtype=jnp.int32)
    counts = jnp.sum((flat_e[:, None] == experts[None, :]).astype(jnp.int32), axis=0)
    padded = (counts + tb - 1) // tb * tb
    pad_end = jnp.cumsum(padded)
    need_end = jnp.cumsum(padded - counts)
    fill = jnp.arange(cap - n_assign, dtype=jnp.int32)
    fill_e = jnp.sum((need_end[None, :] <= fill[:, None]).astype(jnp.int32), axis=1)
    keys = jnp.concatenate([flat_e, fill_e])
    a_id = jnp.concatenate([jnp.arange(n_assign, dtype=jnp.int32),
                            n_assign + jnp.arange(cap - n_assign, dtype=jnp.int32)])
    _, slot_a = lax.sort((keys, a_id), num_keys=1, is_stable=True)
    slot_tok = jnp.where(slot_a < n_assign, slot_a // 2, 0).astype(jnp.int32)
    _, slot_of = lax.sort((slot_a, jnp.arange(cap, dtype=jnp.int32)), num_keys=1)
    slot_of = slot_of[:n_assign]
    starts = jnp.arange(n_blocks, dtype=jnp.int32) * tb
    block_e = jnp.minimum(jnp.sum((pad_end[None, :] <= starts[:, None]).astype(jnp.int32), axis=1),
                          N_EXPERTS - 1).astype(jnp.int32)
    block_nv = jnp.sum((slot_a.reshape(n_blocks, tb) < n_assign).astype(jnp.int32), axis=1)
    pos = slot_of.reshape(n // tm, tm, 2).transpose(0, 2, 1).reshape(n // tm, 1, 2 * tm)
    return block_e, block_nv, slot_tok, pos


def _pad_rows(a, rows=SUBLANES):
    return jnp.concatenate([a, jnp.zeros((rows - a.shape[0],) + a.shape[1:], a.dtype)], axis=0)


def _layer(x_first, x_rest, mem, w_in, hgrn_lb, hgrn_norm_w, gdn_conv_w, gdn_a_log, gdn_dt_bias, gdn_norm_w, w_out,
           ln1_g, ln1_b, xa_w_q, xa_w_kv, xa_w_o, ln2_g, ln2_b, moe_w_group, moe_b_group,
           moe_w_expert, moe_b_expert, moe_w_gate, moe_w_up, moe_w_down, ln3_g, ln3_b,
           *, tm_mm=1024, tbk=512, tm_row=256):
    nb_first, t, _ = x_first.shape
    nb = nb_first + x_rest.shape[0]
    xa = x_first.reshape(nb_first * t, D_MODEL)
    xb = x_rest.reshape((nb - nb_first) * t, D_MODEL)
    layer = 0

    w_in_b = w_in[layer].astype(BF16)
    n_main = 9 * SEG
    w_small = jnp.concatenate([w_in_b[:, n_main:], jnp.zeros((D_MODEL, LANES - 4 * N_HEADS), BF16)], axis=1)
    proj = _matmul_rows2(xa, xb, w_in_b[:, :n_main], tm_mm, SEG, F32)
    ps = _matmul_rows2(xa, xb, w_small, tm_mm, LANES, F32)

    lb2 = jnp.cumsum(jax.nn.softmax(hgrn_lb.astype(F32), axis=0), axis=0)[layer]
    oh = _hgrn2(proj, lb2, nb, t, HG_TBK, HG_HPG)

    qkv = _gdn_prep(proj, gdn_conv_w[layer, :, 0, :].astype(F32), nb, t, tbk)
    coef = jnp.stack([-jnp.exp(gdn_a_log[layer].astype(F32)).reshape(-1),
                      gdn_dt_bias[layer].astype(F32).reshape(-1)], axis=0)
    og = _gdn(qkv, ps, ps.T, coef, nb, t, GD_TBK, GD_HPG)

    nw = _pad_rows(jnp.stack([hgrn_norm_w[layer], gdn_norm_w[layer]], axis=0).astype(F32))
    ln1 = _pad_rows(jnp.stack([ln1_g[layer], ln1_b[layer]], axis=0).astype(F32))
    x1 = _mixout(oh, og, proj, xa, xb, w_out[layer].astype(BF16), nw, ln1, tm_row)

    n_mem = mem.shape[1]
    kv = _matmul(mem.reshape(nb * n_mem, D_MODEL).astype(BF16), xa_w_kv[layer].astype(BF16),
                 n_mem, 1024, BF16)
    ln2 = _pad_rows(jnp.stack([ln2_g[layer], ln2_b[layer]], axis=0).astype(F32))
    w_r = jnp.concatenate([moe_w_expert[layer], moe_w_group[layer],
                           jnp.zeros((D_MODEL, LANES - N_EXPERTS - N_GROUPS), F32)], axis=1).astype(F32)
    w_r_hi = w_r.astype(BF16)
    w_r_mid = (w_r - w_r_hi.astype(F32)).astype(BF16)
    wr = jnp.concatenate([w_r_hi, w_r_mid], axis=1)
    b_r = jnp.concatenate([moe_b_expert[layer], moe_b_group[layer],
                           jnp.zeros((LANES - N_EXPERTS - N_GROUPS,), F32)]).astype(F32)
    br = jnp.broadcast_to(b_r[None, :], (SUBLANES, LANES))
    x2, r = _attn(x1, kv, xa_w_q[layer].astype(BF16), xa_w_o[layer].astype(BF16), ln2, wr, br, nb, t, tm_row)

    e1 = r[:, 0].astype(jnp.int32)
    e2 = r[:, 1].astype(jnp.int32)
    block_e, block_nv, slot_tok, pos = _route_slots(e1, e2, MOE_TB, tm_row)
    y_slots = _moe(x2, block_e, block_nv, slot_tok, moe_w_gate[layer].astype(F32),
                   moe_w_up[layer].astype(F32), moe_w_down[layer].astype(F32))

    ln3 = _pad_rows(jnp.stack([ln3_g[layer], ln3_b[layer]], axis=0).astype(F32))
    y_first, y_rest = _combine(y_slots, pos, x2, r, ln3, tm_row, nb_first * t)
    return y_first.reshape(nb_first, t, D_MODEL), y_rest.reshape(nb - nb_first, t, D_MODEL)


def kernel(x_prompt, x_sample, mem_prompt, mem_sample, w_in, hgrn_lb, hgrn_norm_w, gdn_conv_w, gdn_a_log, gdn_dt_bias, gdn_norm_w, w_out, ln1_g, ln1_b, xa_w_q, xa_w_kv, xa_w_o, ln2_g, ln2_b, moe_w_group, moe_b_group, moe_w_expert, moe_b_expert, moe_w_gate, moe_w_up, moe_w_down, ln3_g, ln3_b):
    assert x_prompt.shape[1] == x_sample.shape[1]
    mem = jnp.concatenate([mem_prompt, mem_sample], axis=0)
    return _layer(x_prompt, x_sample, mem, w_in, hgrn_lb, hgrn_norm_w, gdn_conv_w, gdn_a_log, gdn_dt_bias,
                  gdn_norm_w, w_out, ln1_g, ln1_b, xa_w_q, xa_w_kv, xa_w_o, ln2_g, ln2_b, moe_w_group, moe_b_group,
                  moe_w_expert, moe_b_expert, moe_w_gate, moe_w_up, moe_w_down, ln3_g, ln3_b)
```

```python
import functools
import math

import numpy as np
import jax
import jax.numpy as jnp
from jax import lax
from jax.experimental import pallas as pl
from jax.experimental.pallas import tpu as pltpu

F32 = jnp.float32
BF16 = jnp.bfloat16

D_MODEL = 2048
N_HEADS = 8
D_HEAD = 128
SEG = N_HEADS * D_HEAD
CONV_K = 5
XA_HEADS = 4
XA_HEAD_DIM = D_MODEL // XA_HEADS
N_GROUPS = 4
EXPERTS_PER_GROUP = 8
N_EXPERTS = N_GROUPS * EXPERTS_PER_GROUP
D_EXPERT = 512
DN_ALPHA = 2.0 ** 0.25
LN_EPS = 1e-5
NORM_EPS = 1e-6
LOG2E = math.log2(math.e)

LANES = 128
SUBLANES = 8
VMEM_LIMIT = 56 * 1024 * 1024

HG_CHUNK = 128
HG_TBK = 256
HG_HPG = 8
HG_SMALL_LEVELS = 3
GD_CHUNK = 128
GD_TBK = 256
GD_HPG = 8
MOE_TB = 256


def _cparams(sem):
    return pltpu.CompilerParams(dimension_semantics=sem, vmem_limit_bytes=VMEM_LIMIT)


def _dot(a, b):
    return jnp.dot(a, b, preferred_element_type=F32)


def _dot_nt(a, b):
    return lax.dot_general(a, b, (((1,), (1,)), ((), ())), preferred_element_type=F32)


def _dot_tn(a, b):
    return lax.dot_general(a, b, (((0,), (0,)), ((), ())), preferred_element_type=F32)


def _split2(x):
    hi = x.astype(BF16)
    mid = (x - hi.astype(F32)).astype(BF16)
    return hi, mid


def _sigmoid(x):
    return 1.0 / (1.0 + jnp.exp(-x))


def _layernorm_rows(y, g, b):
    mu = jnp.mean(y, axis=-1, keepdims=True)
    yc = y - mu
    var = jnp.mean(yc * yc, axis=-1, keepdims=True)
    return yc * lax.rsqrt(var + LN_EPS) * g + b


def _mm_kernel(x_ref, w_ref, o_ref):
    o_ref[...] = _dot(x_ref[...], w_ref[...]).astype(o_ref.dtype)


def _matmul(x, w, tm, tn, out_dtype):
    m, k = x.shape
    n = w.shape[1]
    return pl.pallas_call(
        _mm_kernel,
        grid=(m // tm, n // tn),
        in_specs=[pl.BlockSpec((tm, k), lambda i, j: (i, 0)),
                  pl.BlockSpec((k, tn), lambda i, j: (0, j))],
        out_specs=pl.BlockSpec((tm, tn), lambda i, j: (i, j)),
        out_shape=jax.ShapeDtypeStruct((m, n), out_dtype),
        compiler_params=_cparams(("parallel", "arbitrary")),
        name="dense_matmul",
    )(x, w)


def _mm2_kernel(xa_ref, xb_ref, w_ref, ws_ref, o_ref, os_ref, *, na):
    i = pl.program_id(0)
    j = pl.program_id(1)

    def project(x_ref):
        xb16 = x_ref[...].astype(BF16)
        o_ref[...] = _dot(xb16, w_ref[...])

        @pl.when(j == 0)
        def _():
            os_ref[...] = _dot(xb16, ws_ref[...])

    @pl.when(i < na)
    def _():
        project(xa_ref)

    @pl.when(i >= na)
    def _():
        project(xb_ref)


def _in_proj(xa, xb, w, w_small, tm, tn):
    k = xa.shape[1]
    n = w.shape[1]
    ns = w_small.shape[1]
    na = xa.shape[0] // tm
    nt = na + xb.shape[0] // tm
    return pl.pallas_call(
        functools.partial(_mm2_kernel, na=na),
        grid=(nt, n // tn),
        in_specs=[pl.BlockSpec((tm, k), lambda i, j: (jnp.minimum(i, na - 1), 0)),
                  pl.BlockSpec((tm, k), lambda i, j: (jnp.maximum(i - na, 0), 0)),
                  pl.BlockSpec((k, tn), lambda i, j: (0, j)),
                  pl.BlockSpec((k, ns), lambda i, j: (0, 0))],
        out_specs=[pl.BlockSpec((tm, tn), lambda i, j: (i, j)),
                   pl.BlockSpec((tm, ns), lambda i, j: (i, 0))],
        out_shape=[jax.ShapeDtypeStruct((nt * tm, n), F32), jax.ShapeDtypeStruct((nt * tm, ns), F32)],
        compiler_params=_cparams(("parallel", "arbitrary")),
        name="in_proj_matmul",
    )(xa, xb, w, w_small)


def _hgrn_constants(c):
    nl = int(math.log2(c))
    t = np.arange(c)
    u = np.arange(c)
    blocks = []
    blocks.append((u[None, :] <= t[:, None]).astype(np.float32))
    masks, qrows = [], []
    for l in range(nl):
        cc = 1 << l
        m = (t // (2 * cc)) * (2 * cc) + cc
        upper = ((t >> l) & 1) == 1
        a = np.zeros((c, c), np.float32)
        for r in range(c):
            if upper[r]:
                a[r, m[r]:r + 1] = 1.0
            else:
                a[r, r + 1:m[r]] = 1.0
        if l < HG_SMALL_LEVELS:
            blocks.append(a)
        same = (t[:, None] >> (l + 1)) == (t[None, :] >> (l + 1))
        masks.append((same & upper[:, None] & (~upper)[None, :]).astype(np.float32))
        qrows.append(np.broadcast_to(upper[:, None], (c, D_HEAD)).astype(np.float32))
    blocks.append(np.ones((16, c), np.float32))
    a_f = np.concatenate(blocks, axis=0)
    m_f = np.stack(masks)
    q_f = np.stack(qrows)
    a_b = np.concatenate([b[::-1, ::-1] for b in blocks], axis=0)
    m_b = m_f[:, ::-1, ::-1]
    q_b = q_f[:, ::-1, :]
    return (np.stack([a_f, a_b]), np.stack([m_f, m_b]).copy(), np.stack([q_f, q_b]).copy(), nl)


def _hgrn_kernel(q_ref, f_ref, v_ref, lb_ref, a_ref, m_ref, qr_ref, o_ref, st_ref, *, c, nck, nl, scale, hpg):
    d = pl.program_id(0)
    i = pl.program_id(3)

    @pl.when(i == 0)
    def _():
        st_ref[...] = jnp.zeros_like(st_ref)

    probs = [(j, g) for j in range(nck) for g in range(hpg)]

    def rows(j):
        jj = jnp.where(d == 0, j, nck - 1 - j)
        return pl.ds(pl.multiple_of(jj * c, c), c)

    def cols(g):
        return slice(g * D_HEAD, (g + 1) * D_HEAD)

    def eblk(k, g2, rows=c):
        e2 = _dot(a_ref[pl.ds(k * c, rows), :], g2)
        return e2[:, :D_HEAD] + e2[:, D_HEAD:]

    def boundary(gcum, l):
        cc = 1 << l
        pieces = []
        for p in range(c // (2 * cc)):
            m = p * 2 * cc + cc
            rowv = jnp.where(d == 0, gcum[m - 1:m, :], gcum[m:m + 1, :])
            pieces.append(jnp.broadcast_to(rowv, (2 * cc, D_HEAD)))
        return pieces[0] if len(pieces) == 1 else jnp.concatenate(pieces, axis=0)

    qs, kk, vb, vv, g2, gcum, gtot = {}, {}, {}, {}, {}, {}, {}
    for (j, g) in probs:
        lb = lb_ref[g, 0:1, :]
        one_m_lb = 1.0 - lb
        fr = f_ref[rows(j), cols(g)]
        qs[j, g] = q_ref[rows(j), cols(g)] * scale
        v = v_ref[rows(j), cols(g)]
        vv[j, g] = v
        vb[j, g] = v.astype(BF16)
        gl = jnp.log(lb + one_m_lb * _sigmoid(fr)) * LOG2E
        kk[j, g] = one_m_lb * _sigmoid(-fr)
        g_hi, g_mid = _split2(gl)
        g2[j, g] = jnp.concatenate([g_hi, g_mid], axis=1)
    for p in probs:
        gcum[p] = eblk(0, g2[p])
        gtot[p] = eblk(1 + HG_SMALL_LEVELS, g2[p], rows=16)[0:1, :]

    s = {}
    for l in range(nl):
        if l < HG_SMALL_LEVELS:
            e = {p: eblk(1 + l, g2[p]) for p in probs}
        else:
            e = {p: -jnp.abs(gcum[p] - boundary(gcum[p], l)) for p in probs}
        w = {p: (jnp.where(qr_ref[l] > 0.5, qs[p], kk[p]) * jnp.exp2(e[p])).astype(BF16) for p in probs}
        sd = {p: _dot_nt(w[p], w[p]) for p in probs}
        for p in probs:
            s[p] = m_ref[l] * sd[p] if l == 0 else s[p] + m_ref[l] * sd[p]

    o_in, qd, upd, dec = {}, {}, {}, {}
    for p in probs:
        dg = jnp.sum(qs[p] * kk[p], axis=-1, keepdims=True)
        o_in[p] = _dot(s[p].astype(BF16), vb[p]) + dg * vv[p]
        qd[p] = (qs[p] * jnp.exp2(gcum[p])).astype(BF16)
        kd = (kk[p] * jnp.exp2(gtot[p] - gcum[p])).astype(BF16)
        upd[p] = _dot_tn(vb[p], kd)
        dec[p] = jnp.exp2(gtot[p])

    st = [st_ref[g] for g in range(hpg)]
    for j in range(nck):
        for g in range(hpg):
            p = (j, g)
            o_ref[rows(j), cols(g)] = (o_in[p] + _dot_nt(qd[p], st[g].astype(BF16))).astype(o_ref.dtype)
            st[g] = st[g] * dec[p] + upd[p]
    for g in range(hpg):
        st_ref[g] = st[g]


def _hgrn2(proj, lb2, nb, t, tbk, hpg):
    c = HG_CHUNK
    n = nb * t
    nblk = t // tbk
    a_np, m_np, q_np, nl = _hgrn_constants(c)
    a_all = jnp.asarray(a_np, BF16)
    masks = jnp.asarray(m_np, F32)
    qrows = jnp.asarray(q_np, F32)
    lb8 = jnp.broadcast_to(lb2.reshape(2, N_HEADS, 1, D_HEAD), (2, N_HEADS, SUBLANES, D_HEAD))
    ngrp = N_HEADS // hpg
    wb = hpg * D_HEAD

    def row(d, b, i):
        return b * nblk + jnp.where(d == 0, i, nblk - 1 - i)

    kern = functools.partial(_hgrn_kernel, c=c, nck=tbk // c, nl=nl, scale=D_HEAD ** -0.5, hpg=hpg)
    na = a_np.shape[1]
    return pl.pallas_call(
        kern,
        grid=(2, nb, ngrp, nblk),
        in_specs=[
            pl.BlockSpec((tbk, wb), lambda d, b, h, i: (row(d, b, i), h)),
            pl.BlockSpec((tbk, wb), lambda d, b, h, i: (row(d, b, i), (1 + d) * ngrp + h)),
            pl.BlockSpec((tbk, wb), lambda d, b, h, i: (row(d, b, i), 3 * ngrp + h)),
            pl.BlockSpec((None, hpg, SUBLANES, D_HEAD), lambda d, b, h, i: (d, h, 0, 0)),
            pl.BlockSpec((None, na, c), lambda d, b, h, i: (d, 0, 0)),
            pl.BlockSpec((None, nl, c, c), lambda d, b, h, i: (d, 0, 0, 0)),
            pl.BlockSpec((None, nl, c, D_HEAD), lambda d, b, h, i: (d, 0, 0, 0)),
        ],
        out_specs=pl.BlockSpec((None, tbk, wb), lambda d, b, h, i: (d, row(d, b, i), h)),
        out_shape=jax.ShapeDtypeStruct((2, n, SEG), BF16),
        scratch_shapes=[pltpu.VMEM((hpg, D_HEAD, D_HEAD), F32)],
        compiler_params=_cparams(("parallel", "parallel", "parallel", "arbitrary")),
        name="hgrn2_bidir",
    )(proj, proj, proj, lb8, a_all, masks, qrows)


def _gdn_prep_kernel(prev_ref, cur_ref, nxt_ref, w_ref, o_ref, *, tbk, scale):
    i = pl.program_id(1)
    s = pl.program_id(2)
    nblk = pl.num_programs(1)
    prev = jnp.where(i == 0, 0.0, prev_ref[...])
    nxt = jnp.where(i == nblk - 1, 0.0, nxt_ref[...])
    full = jnp.concatenate([prev, cur_ref[...], nxt], axis=0)
    rows = tbk + 2 * SUBLANES
    acc = jnp.zeros((tbk, SEG), F32)
    for j in range(CONV_K):
        shift = (CONV_K // 2 - j) % rows
        sh = full if shift == 0 else pltpu.roll(full, shift, axis=0)
        acc = acc + sh[SUBLANES:SUBLANES + tbk, :] * w_ref[j:j + 1, :]
    y = acc * _sigmoid(acc)
    fac_all = jnp.where(s == 0, scale, 1.0)
    for h in range(N_HEADS):
        ys = y[:, h * D_HEAD:(h + 1) * D_HEAD]
        ss = jnp.sum(ys * ys, axis=-1, keepdims=True)
        fac = jnp.where(s < 2, lax.rsqrt(ss + NORM_EPS), 1.0) * fac_all
        o_ref[:, h * D_HEAD:(h + 1) * D_HEAD] = ys * fac


def _gdn_prep(proj, conv_w, nb, t, tbk):
    n = nb * t
    nblk = t // tbk
    hb = tbk // SUBLANES
    w8 = jnp.concatenate([conv_w, jnp.zeros((SUBLANES - CONV_K, 3 * SEG), F32)], axis=0)
    last = n // SUBLANES - 1
    kern = functools.partial(_gdn_prep_kernel, tbk=tbk, scale=D_HEAD ** -0.5)
    return pl.pallas_call(
        kern,
        grid=(nb, nblk, 3),
        in_specs=[
            pl.BlockSpec((SUBLANES, SEG), lambda b, i, s: (jnp.maximum((b * nblk + i) * hb - 1, 0), 5 + s)),
            pl.BlockSpec((tbk, SEG), lambda b, i, s: (b * nblk + i, 5 + s)),
            pl.BlockSpec((SUBLANES, SEG), lambda b, i, s: (jnp.minimum((b * nblk + i + 1) * hb, last), 5 + s)),
            pl.BlockSpec((SUBLANES, SEG), lambda b, i, s: (0, s)),
        ],
        out_specs=pl.BlockSpec((None, tbk, SEG), lambda b, i, s: (s, b * nblk + i, 0)),
        out_shape=jax.ShapeDtypeStruct((3, n, SEG), F32),
        compiler_params=_cparams(("parallel", "parallel", "arbitrary")),
        name="gdn_conv_prep",
    )(proj, proj, proj, w8)


def _gdn_constants(c):
    nl = int(math.log2(c))
    t = np.arange(c)
    incl = (t[None, :] <= t[:, None]).astype(np.float32)
    strict = (t[None, :] < t[:, None]).astype(np.float32)
    after = (t[:, None] > t[None, :]).astype(np.float32)
    rhs_f = np.concatenate([np.ones((c, LANES), np.float32), after], axis=1)
    lvl = []
    for l in range(nl):
        upper = ((t >> l) & 1) == 1
        same = (t[:, None] >> (l + 1)) == (t[None, :] >> (l + 1))
        lvl.append((same & upper[:, None] & (~upper)[None, :]).astype(np.float32))
    lvl = np.stack(lvl)
    mk_f = np.concatenate([incl[None], strict[None], lvl], axis=0)
    mk_b = mk_f[:, ::-1, ::-1]
    rhs_b = np.concatenate([np.ones((c, LANES), np.float32), after[::-1, ::-1]], axis=1)
    return np.stack([mk_f, mk_b]).copy(), np.stack([rhs_f, rhs_b]).copy(), nl


def _gdn_kernel(coef_ref, q_ref, k_ref, v_ref, arow_ref, ps_ref, mk_ref, rhs_ref, o_ref, st_ref, *, c, nck, nl, hpg):
    d = pl.program_id(0)
    hg = pl.program_id(2)
    i = pl.program_id(3)

    @pl.when(i == 0)
    def _():
        st_ref[...] = jnp.zeros_like(st_ref)

    lane = lax.broadcasted_iota(jnp.int32, (c, LANES), 1)
    incl = mk_ref[0]
    strict = mk_ref[1]
    eye = incl - strict
    rhs = rhs_ref[...]
    probs = [(j, g) for j in range(nck) for g in range(hpg)]

    def rows(j):
        jj = jnp.where(d == 0, j, nck - 1 - j)
        return pl.ds(pl.multiple_of(jj * c, c), c)

    def cols(g):
        return slice(g * D_HEAD, (g + 1) * D_HEAD)

    qn, kn, vv, beta, gcum, dec, etot = {}, {}, {}, {}, {}, {}, {}
    for (j, g) in probs:
        h = hg * hpg + g
        neg_ea = coef_ref[0, d * N_HEADS + h]
        dtb = coef_ref[1, d * N_HEADS + h]
        a_blk = arow_ref[pl.ds(h, 1), :]
        a_row = jnp.where(d == 0, a_blk[:, j * c:(j + 1) * c], a_blk[:, (nck - 1 - j) * c:(nck - j) * c])
        z = a_row + dtb
        g_row = neg_ea * (jnp.maximum(z, 0.0) + jnp.log(1.0 + jnp.exp(-jnp.abs(z))))
        beta_lane = 2 * N_HEADS + d * N_HEADS + h
        b_col = jnp.sum(jnp.where(lane == beta_lane, ps_ref[rows(j), :], 0.0), axis=-1, keepdims=True)
        beta[j, g] = _sigmoid(b_col)
        lg = incl * g_row
        lg_hi, lg_mid = _split2(lg)
        out = _dot(jnp.concatenate([lg_hi, lg_mid], axis=0), rhs)
        out = out[:c] + out[c:]
        gcum[j, g] = out[:, :LANES]
        dmat = out[:, LANES:]
        dec[j, g] = jnp.exp2(jnp.minimum(dmat, 0.0)) * incl
        qn[j, g] = q_ref[rows(j), cols(g)]
        kn[j, g] = k_ref[rows(j), cols(g)]
        vv[j, g] = v_ref[rows(j), cols(g)]

    kb, knb, a_dec, tinv = {}, {}, {}, {}
    for p in probs:
        kb[p] = kn[p] * beta[p]
        knb[p] = kn[p].astype(BF16)
        a_dec[p] = _dot_nt(kb[p].astype(BF16), knb[p]) * dec[p]
        tinv[p] = eye - a_dec[p] * mk_ref[2]
    for l in range(1, nl):
        ml = mk_ref[2 + l]
        tb = {p: tinv[p].astype(BF16) for p in probs}
        ta = {p: _dot(tb[p], (a_dec[p] * ml).astype(BF16)).astype(BF16) for p in probs}
        for p in probs:
            tinv[p] = tinv[p] - _dot(ta[p], tb[p])

    u, w, qk, qe, ke = {}, {}, {}, {}, {}
    for p in probs:
        gtot = jnp.where(d == 0, gcum[p][c - 1:c, :], gcum[p][0:1, :])
        eg = jnp.exp2(gcum[p])
        etot[p] = jnp.exp2(gtot)
        uw = _dot(tinv[p].astype(BF16), jnp.concatenate([vv[p] * beta[p], kb[p] * eg], axis=1).astype(BF16))
        u[p] = uw[:, :D_HEAD]
        w[p] = uw[:, D_HEAD:].astype(BF16)
        qk[p] = (_dot_nt(qn[p].astype(BF16), knb[p]) * dec[p]).astype(BF16)
        qe[p] = (qn[p] * eg).astype(BF16)
        ke[p] = (kn[p] * jnp.exp2(gtot - gcum[p])).astype(BF16)

    st = [st_ref[g] for g in range(hpg)]
    for j in range(nck):
        stb = [st[g].astype(BF16) for g in range(hpg)]
        ws = [_dot(w[j, g], stb[g]) for g in range(hpg)]
        qs = [_dot(qe[j, g], stb[g]) for g in range(hpg)]
        vnb = [(u[j, g] - ws[g]).astype(BF16) for g in range(hpg)]
        for g in range(hpg):
            o_ref[rows(j), cols(g)] = (qs[g] + _dot(qk[j, g], vnb[g])).astype(o_ref.dtype)
        for g in range(hpg):
            st[g] = st[g] * etot[j, g] + _dot_tn(ke[j, g], vnb[g])
    for g in range(hpg):
        st_ref[g] = st[g]


def _gdn(qkv, ps, ps_t, coef, nb, t, tbk, hpg):
    c = GD_CHUNK
    n = nb * t
    nblk = t // tbk
    mk_np, rhs_np, nl = _gdn_constants(c)
    mk = jnp.asarray(mk_np, F32)
    rhs = jnp.asarray(rhs_np, BF16)
    wb = hpg * D_HEAD

    def row(d, b, i):
        return b * nblk + jnp.where(d == 0, i, nblk - 1 - i)

    kern = functools.partial(_gdn_kernel, c=c, nck=tbk // c, nl=nl, hpg=hpg)
    grid_spec = pltpu.PrefetchScalarGridSpec(
        num_scalar_prefetch=0,
        grid=(2, nb, N_HEADS // hpg, nblk),
        in_specs=[
            pl.BlockSpec(memory_space=pltpu.SMEM),
            pl.BlockSpec((None, tbk, wb), lambda d, b, h, i: (0, row(d, b, i), h)),
            pl.BlockSpec((None, tbk, wb), lambda d, b, h, i: (1, row(d, b, i), h)),
            pl.BlockSpec((None, tbk, wb), lambda d, b, h, i: (2, row(d, b, i), h)),
            pl.BlockSpec((SUBLANES, tbk), lambda d, b, h, i: (d, row(d, b, i))),
            pl.BlockSpec((tbk, LANES), lambda d, b, h, i: (row(d, b, i), 0)),
            pl.BlockSpec((None, 2 + nl, c, c), lambda d, b, h, i: (d, 0, 0, 0)),
            pl.BlockSpec((None, c, c + LANES), lambda d, b, h, i: (d, 0, 0)),
        ],
        out_specs=pl.BlockSpec((None, tbk, wb), lambda d, b, h, i: (d, row(d, b, i), h)),
        scratch_shapes=[pltpu.VMEM((hpg, D_HEAD, D_HEAD), F32)],
    )
    return pl.pallas_call(
        kern,
        grid_spec=grid_spec,
        out_shape=jax.ShapeDtypeStruct((2, n, SEG), BF16),
        compiler_params=_cparams(("parallel", "parallel", "parallel", "arbitrary")),
        name="gdn_bidir",
    )(coef, qkv, qkv, qkv, ps_t, ps, mk, rhs)


def _mixout_kernel(oh_ref, og_ref, hg_ref, gz_ref, xa_ref, xb_ref, w_ref, nw_ref, ln_ref, o_ref, lhs_ref, *, na):
    i = pl.program_id(0)

    def gated(o2_ref, gate_ref, wrow, col0):
        for h in range(N_HEADS):
            sl = slice(h * D_HEAD, (h + 1) * D_HEAD)
            o = o2_ref[0, :, sl].astype(F32) + o2_ref[1, :, sl].astype(F32)
            ms = jnp.mean(o * o, axis=-1, keepdims=True)
            gt = gate_ref[:, sl]
            y = o * lax.rsqrt(ms + NORM_EPS) * wrow * (gt * _sigmoid(gt))
            lhs_ref[:, col0 + h * D_HEAD:col0 + (h + 1) * D_HEAD] = y.astype(BF16)

    gated(oh_ref, hg_ref, nw_ref[0:1, :], 0)
    gated(og_ref, gz_ref, nw_ref[1:2, :], SEG)
    hmix = _dot(lhs_ref[...], w_ref[...])
    x = jnp.where(i < na, xa_ref[...], xb_ref[...])
    y = DN_ALPHA * x + hmix
    o_ref[...] = _layernorm_rows(y, ln_ref[0:1, :], ln_ref[1:2, :])


def _mixout(oh, og, proj, xa, xb, w_out_b, nw, ln, tm):
    na = xa.shape[0] // tm
    n = xa.shape[0] + xb.shape[0]
    return pl.pallas_call(
        functools.partial(_mixout_kernel, na=na),
        grid=(n // tm,),
        in_specs=[
            pl.BlockSpec((2, tm, SEG), lambda i: (0, i, 0)),
            pl.BlockSpec((2, tm, SEG), lambda i: (0, i, 0)),
            pl.BlockSpec((tm, SEG), lambda i: (i, 4)),
            pl.BlockSpec((tm, SEG), lambda i: (i, 8)),
            pl.BlockSpec((tm, D_MODEL), lambda i: (jnp.minimum(i, na - 1), 0)),
            pl.BlockSpec((tm, D_MODEL), lambda i: (jnp.maximum(i - na, 0), 0)),
            pl.BlockSpec((2 * SEG, D_MODEL), lambda i: (0, 0)),
            pl.BlockSpec((SUBLANES, D_HEAD), lambda i: (0, 0)),
            pl.BlockSpec((SUBLANES, D_MODEL), lambda i: (0, 0)),
        ],
        out_specs=pl.BlockSpec((tm, D_MODEL), lambda i: (i, 0)),
        out_shape=jax.ShapeDtypeStruct((n, D_MODEL), F32),
        scratch_shapes=[pltpu.VMEM((tm, 2 * SEG), BF16)],
        compiler_params=_cparams(("parallel",)),
        name="mixer_out_ln1",
    )(oh, og, proj, proj, xa, xb, w_out_b, nw, ln)


def _attn_kernel(x_ref, kv_ref, wq_ref, wo_ref, ln_ref, wr_ref, br_ref, o_ref, r_ref, att_ref):
    x1 = x_ref[...]
    q = _dot(x1.astype(BF16), wq_ref[...])
    sc = XA_HEAD_DIM ** -0.5
    for h in range(XA_HEADS):
        sl = slice(h * XA_HEAD_DIM, (h + 1) * XA_HEAD_DIM)
        kh = kv_ref[:, sl]
        vh = kv_ref[:, D_MODEL + h * XA_HEAD_DIM:D_MODEL + (h + 1) * XA_HEAD_DIM]
        s = _dot_nt(q[:, sl].astype(BF16), kh) * sc
        s = s - jnp.max(s, axis=-1, keepdims=True)
        p = jnp.exp(s)
        p = p / jnp.sum(p, axis=-1, keepdims=True)
        att_ref[:, sl] = _dot(p.astype(BF16), vh).astype(BF16)
    hx = _dot(att_ref[...], wo_ref[...])
    x2 = _layernorm_rows(DN_ALPHA * x1 + hx, ln_ref[0:1, :], ln_ref[1:2, :])
    o_ref[...] = x2

    x_hi, x_mid = _split2(x2)
    lg2 = _dot(x_hi, wr_ref[...])
    lg = lg2[:, :LANES] + lg2[:, LANES:] + _dot(x_mid, wr_ref[:, :LANES]) + br_ref[0:1, :]
    tm = lg.shape[0]
    lane_i = lax.broadcasted_iota(jnp.int32, (tm, LANES), 1)
    lane = lane_i.astype(F32)
    grp_of_lane = (lane_i >> 3).astype(F32)
    neg = jnp.float32(-1e30)
    big = jnp.float32(LANES)
    is_g = (lane_i >= N_EXPERTS) & (lane_i < N_EXPERTS + N_GROUPS)
    gl = jnp.where(is_g, lg, neg)
    gmax = jnp.max(gl, axis=-1, keepdims=True)
    gsel = jnp.min(jnp.where(gl == gmax, lane, big), axis=-1, keepdims=True) - N_EXPERTS
    p_group = 1.0 / jnp.sum(jnp.where(is_g, jnp.exp(gl - gmax), 0.0), axis=-1, keepdims=True)
    in_grp = (lane_i < N_EXPERTS) & (grp_of_lane == gsel)
    el = jnp.where(in_grp, lg, neg)
    m1 = jnp.max(el, axis=-1, keepdims=True)
    i1 = jnp.min(jnp.where(el == m1, lane, big), axis=-1, keepdims=True)
    el2 = jnp.where(lane == i1, neg, el)
    m2 = jnp.max(el2, axis=-1, keepdims=True)
    i2 = jnp.min(jnp.where(el2 == m2, lane, big), axis=-1, keepdims=True)
    e2 = jnp.exp(m2 - m1)
    g1 = p_group / (1.0 + e2)
    g2 = p_group * e2 / (1.0 + e2)
    r = jnp.where(lane_i == 0, i1, 0.0)
    r = jnp.where(lane_i == 1, i2, r)
    r = jnp.where(lane_i == 2, g1, r)
    r = jnp.where(lane_i == 3, g2, r)
    r_ref[...] = r


def _attn(x1, kv, wq_b, wo_b, ln, wr, br, nb, t, tm):
    n = nb * t
    nblk = t // tm
    n_mem = kv.shape[0] // nb
    return pl.pallas_call(
        _attn_kernel,
        grid=(nb, nblk),
        in_specs=[
            pl.BlockSpec((tm, D_MODEL), lambda b, i: (b * nblk + i, 0)),
            pl.BlockSpec((n_mem, 2 * D_MODEL), lambda b, i: (b, 0)),
            pl.BlockSpec((D_MODEL, D_MODEL), lambda b, i: (0, 0)),
            pl.BlockSpec((D_MODEL, D_MODEL), lambda b, i: (0, 0)),
            pl.BlockSpec((SUBLANES, D_MODEL), lambda b, i: (0, 0)),
            pl.BlockSpec((D_MODEL, 2 * LANES), lambda b, i: (0, 0)),
            pl.BlockSpec((SUBLANES, LANES), lambda b, i: (0, 0)),
        ],
        out_specs=[pl.BlockSpec((tm, D_MODEL), lambda b, i: (b * nblk + i, 0)),
                   pl.BlockSpec((tm, LANES), lambda b, i: (b * nblk + i, 0))],
        out_shape=[jax.ShapeDtypeStruct((n, D_MODEL), F32), jax.ShapeDtypeStruct((n, LANES), F32)],
        scratch_shapes=[pltpu.VMEM((tm, D_MODEL), BF16)],
        compiler_params=_cparams(("parallel", "arbitrary")),
        name="xattn_ln2_router",
    )(x1, kv, wq_b, wo_b, ln, wr, br)


DMA_UNROLL = 8


def _issue_row_gather(idx_ref, src_hbm, dst, sem, n_rows):
    def body(k, carry):
        for u in range(DMA_UNROLL):
            r = k * DMA_UNROLL + u
            pltpu.make_async_copy(src_hbm.at[pl.ds(idx_ref[0, r], 1)], dst.at[pl.ds(r, 1)], sem).start(priority=u % 2)
        return carry

    lax.fori_loop(0, n_rows // DMA_UNROLL, body, 0)


def _issue_row_gather_inline(idx_ref, src_hbm, dst, sem, n_rows):
    for r in range(n_rows):
        pltpu.make_async_copy(src_hbm.at[pl.ds(idx_ref[0, r], 1)], dst.at[pl.ds(r, 1)], sem).start(priority=r % 2)


def _wait_row_gather(src_hbm, dst, sem, n_rows):
    pltpu.make_async_copy(src_hbm.at[pl.ds(0, n_rows)], dst, sem).wait()


def _moe_kernel(be_ref, nv_ref, tok_ref, tokn_ref, x_hbm, wg_ref, wu_ref, wd_ref, o_ref, xbuf, sem,
                wgb, wub, wdb, *, tb):
    b = pl.program_id(0)
    nb = pl.num_programs(0)
    slot = b % 2

    @pl.when((b == 0) | (be_ref[b] != be_ref[jnp.maximum(b - 1, 0)]))
    def _():
        wgb[...] = wg_ref[...].astype(BF16)
        wub[...] = wu_ref[...].astype(BF16)
        wdb[...] = wd_ref[...].astype(BF16)

    @pl.when((b == 0) & (nv_ref[0] > 0))
    def _():
        _issue_row_gather(tok_ref, x_hbm, xbuf.at[0], sem.at[0], tb)

    has_next = (b + 1 < nb) & (nv_ref[jnp.minimum(b + 1, nb - 1)] > 0)

    def compute():
        xb = xbuf[slot].astype(BF16)
        h1 = _dot(xb, wgb[...])
        h2 = _dot(xb, wub[...])
        hdn = (h1 * _sigmoid(h1) * h2).astype(BF16)
        o_ref[...] = _dot(hdn, wdb[...])

    @pl.when((nv_ref[b] > 0) & has_next)
    def _():
        _wait_row_gather(x_hbm, xbuf.at[slot], sem.at[slot], tb)
        _issue_row_gather_inline(tokn_ref, x_hbm, xbuf.at[1 - slot], sem.at[1 - slot], tb)
        compute()

    @pl.when((nv_ref[b] > 0) & jnp.logical_not(has_next))
    def _():
        _wait_row_gather(x_hbm, xbuf.at[slot], sem.at[slot], tb)
        compute()

    @pl.when((nv_ref[b] == 0) & has_next)
    def _():
        _issue_row_gather(tokn_ref, x_hbm, xbuf.at[1 - slot], sem.at[1 - slot], tb)

    @pl.when(nv_ref[b] == 0)
    def _():
        o_ref[...] = jnp.zeros_like(o_ref)


def _moe(x2, block_e, block_nv, slot_tok, wg_b, wu_b, wd_b):
    tb = MOE_TB
    cap = slot_tok.shape[0]
    n_blocks = cap // tb
    tok3 = slot_tok.reshape(n_blocks, 1, tb)
    grid_spec = pltpu.PrefetchScalarGridSpec(
        num_scalar_prefetch=2,
        grid=(n_blocks,),
        in_specs=[
            pl.BlockSpec((None, 1, tb), lambda b, be, nv: (b, 0, 0), memory_space=pltpu.SMEM),
            pl.BlockSpec((None, 1, tb), lambda b, be, nv: (jnp.minimum(b + 1, n_blocks - 1), 0, 0),
                         memory_space=pltpu.SMEM),
            pl.BlockSpec(memory_space=pl.ANY),
            pl.BlockSpec((None, D_MODEL, D_EXPERT), lambda b, be, nv: (be[b], 0, 0)),
            pl.BlockSpec((None, D_MODEL, D_EXPERT), lambda b, be, nv: (be[b], 0, 0)),
            pl.BlockSpec((None, D_EXPERT, D_MODEL), lambda b, be, nv: (be[b], 0, 0)),
        ],
        out_specs=pl.BlockSpec((tb, D_MODEL), lambda b, be, nv: (b, 0)),
        scratch_shapes=[pltpu.VMEM((2, tb, D_MODEL), F32), pltpu.SemaphoreType.DMA((2,)),
                        pltpu.VMEM((D_MODEL, D_EXPERT), BF16), pltpu.VMEM((D_MODEL, D_EXPERT), BF16),
                        pltpu.VMEM((D_EXPERT, D_MODEL), BF16)],
    )
    return pl.pallas_call(
        functools.partial(_moe_kernel, tb=tb),
        grid_spec=grid_spec,
        out_shape=jax.ShapeDtypeStruct((cap, D_MODEL), F32),
        compiler_params=_cparams(("arbitrary",)),
        name="moe_grouped_mlp",
    )(block_e, block_nv, tok3, tok3, x2, wg_b, wu_b, wd_b)


def _combine_kernel(pos_ref, posn_ref, y_hbm, x_ref, r_ref, ln_ref, op_ref, os_ref, ybuf, sem, *, tm, npt):
    i = pl.program_id(0)
    nt = pl.num_programs(0)
    slot = i % 2

    @pl.when(i == 0)
    def _():
        _issue_row_gather(pos_ref, y_hbm, ybuf.at[0], sem.at[0], 2 * tm)

    def finish():
        r = r_ref[...]
        y = r[:, 2:3] * ybuf[slot, 0:tm, :] + r[:, 3:4] * ybuf[slot, tm:2 * tm, :]
        res = _layernorm_rows(DN_ALPHA * x_ref[...] + y, ln_ref[0:1, :], ln_ref[1:2, :])

        @pl.when(i < npt)
        def _():
            op_ref[...] = res

        @pl.when(i >= npt)
        def _():
            os_ref[...] = res

    @pl.when(i + 1 < nt)
    def _():
        _wait_row_gather(y_hbm, ybuf.at[slot], sem.at[slot], 2 * tm)
        _issue_row_gather_inline(posn_ref, y_hbm, ybuf.at[1 - slot], sem.at[1 - slot], 2 * tm)
        finish()

    @pl.when(i + 1 >= nt)
    def _():
        _wait_row_gather(y_hbm, ybuf.at[slot], sem.at[slot], 2 * tm)
        finish()


def _combine(y_slots, pos, x2, r, ln, tm, n_first):
    n = x2.shape[0]
    nt = n // tm
    npt = n_first // tm
    grid_spec = pltpu.PrefetchScalarGridSpec(
        num_scalar_prefetch=0,
        grid=(nt,),
        in_specs=[
            pl.BlockSpec((None, 1, 2 * tm), lambda i: (i, 0, 0), memory_space=pltpu.SMEM),
            pl.BlockSpec((None, 1, 2 * tm), lambda i: (jnp.minimum(i + 1, nt - 1), 0, 0), memory_space=pltpu.SMEM),
            pl.BlockSpec(memory_space=pl.ANY),
            pl.BlockSpec((tm, D_MODEL), lambda i: (i, 0)),
            pl.BlockSpec((tm, LANES), lambda i: (i, 0)),
            pl.BlockSpec((SUBLANES, D_MODEL), lambda i: (0, 0)),
        ],
        out_specs=[pl.BlockSpec((tm, D_MODEL), lambda i: (jnp.minimum(i, npt - 1), 0)),
                   pl.BlockSpec((tm, D_MODEL), lambda i: (jnp.maximum(i - npt, 0), 0))],
        scratch_shapes=[pltpu.VMEM((2, 2 * tm, D_MODEL), F32), pltpu.SemaphoreType.DMA((2,))],
    )
    return pl.pallas_call(
        functools.partial(_combine_kernel, tm=tm, npt=npt),
        grid_spec=grid_spec,
        out_shape=[jax.ShapeDtypeStruct((n_first, D_MODEL), F32), jax.ShapeDtypeStruct((n - n_first, D_MODEL), F32)],
        compiler_params=_cparams(("arbitrary",)),
        name="moe_combine_ln3",
    )(pos, pos, y_slots, x2, r, ln)


def _route_slots(e1, e2, tb, tm):
    n = e1.shape[0]
    flat_e = jnp.stack([e1, e2], axis=1).reshape(-1)
    n_assign = 2 * n
    n_blocks = -(-n_assign // tb) + N_EXPERTS
    cap = n_blocks * tb
    experts = jnp.arange(N_EXPERTS, dtype=jnp.int32)
    counts = jnp.sum((flat_e[:, None] == experts[None, :]).astype(jnp.int32), axis=0)
    padded = (counts + tb - 1) // tb * tb
    pad_end = jnp.cumsum(padded)
    need_end = jnp.cumsum(padded - counts)
    fill = jnp.arange(cap - n_assign, dtype=jnp.int32)
    fill_e = jnp.sum((need_end[None, :] <= fill[:, None]).astype(jnp.int32), axis=1)
    keys = jnp.concatenate([flat_e, fill_e])
    a_id = jnp.concatenate([jnp.arange(n_assign, dtype=jnp.int32),
                            n_assign + jnp.arange(cap - n_assign, dtype=jnp.int32)])
    _, slot_a = lax.sort((keys, a_id), num_keys=1, is_stable=True)
    slot_tok = jnp.where(slot_a < n_assign, slot_a // 2, 0).astype(jnp.int32)
    _, slot_of = lax.sort((slot_a, jnp.arange(cap, dtype=jnp.int32)), num_keys=1)
    slot_of = slot_of[:n_assign]
    starts = jnp.arange(n_blocks, dtype=jnp.int32) * tb
    block_e = jnp.minimum(jnp.sum((pad_end[None, :] <= starts[:, None]).astype(jnp.int32), axis=1),
                          N_EXPERTS - 1).astype(jnp.int32)
    block_nv = jnp.sum((slot_a.reshape(n_blocks, tb) < n_assign).astype(jnp.int32), axis=1)
    pos = slot_of.reshape(n // tm, tm, 2).transpose(0, 2, 1).reshape(n // tm, 1, 2 * tm)
    return block_e, block_nv, slot_tok, pos


def _pad_rows(a, rows=SUBLANES):
    return jnp.concatenate([a, jnp.zeros((rows - a.shape[0],) + a.shape[1:], a.dtype)], axis=0)


def _layer(x_first, x_rest, mem, w_in, hgrn_lb, hgrn_norm_w, gdn_conv_w, gdn_a_log, gdn_dt_bias, gdn_norm_w, w_out,
           ln1_g, ln1_b, xa_w_q, xa_w_kv, xa_w_o, ln2_g, ln2_b, moe_w_group, moe_b_group,
           moe_w_expert, moe_b_expert, moe_w_gate, moe_w_up, moe_w_down, ln3_g, ln3_b,
           *, tm_mm=1024, tbk=512, tm_row=256):
    nb_first, t, _ = x_first.shape
    nb = nb_first + x_rest.shape[0]
    xa = x_first.reshape(nb_first * t, D_MODEL)
    xb = x_rest.reshape((nb - nb_first) * t, D_MODEL)
    layer = 0

    w_in_b = w_in[layer].astype(BF16)
    n_main = 9 * SEG
    w_small = jnp.concatenate([w_in_b[:, n_main:], jnp.zeros((D_MODEL, LANES - 4 * N_HEADS), BF16)], axis=1)
    proj, ps = _in_proj(xa, xb, w_in_b[:, :n_main], w_small, tm_mm, SEG)

    lb2 = jnp.cumsum(jax.nn.softmax(hgrn_lb.astype(F32), axis=0), axis=0)[layer]
    oh = _hgrn2(proj, lb2, nb, t, HG_TBK, HG_HPG)

    qkv = _gdn_prep(proj, gdn_conv_w[layer, :, 0, :].astype(F32), nb, t, tbk)
    coef = jnp.stack([-jnp.exp(gdn_a_log[layer].astype(F32)).reshape(-1) * LOG2E,
                      gdn_dt_bias[layer].astype(F32).reshape(-1)], axis=0)
    og = _gdn(qkv, ps, ps.T, coef, nb, t, GD_TBK, GD_HPG)

    nw = _pad_rows(jnp.stack([hgrn_norm_w[layer], gdn_norm_w[layer]], axis=0).astype(F32))
    ln1 = _pad_rows(jnp.stack([ln1_g[layer], ln1_b[layer]], axis=0).astype(F32))
    x1 = _mixout(oh, og, proj, xa, xb, w_out[layer].astype(BF16), nw, ln1, tm_row)

    n_mem = mem.shape[1]
    kv = _matmul(mem.reshape(nb * n_mem, D_MODEL).astype(BF16), xa_w_kv[layer].astype(BF16),
                 n_mem, 1024, BF16)
    ln2 = _pad_rows(jnp.stack([ln2_g[layer], ln2_b[layer]], axis=0).astype(F32))
    w_r = jnp.concatenate([moe_w_expert[layer], moe_w_group[layer],
                           jnp.zeros((D_MODEL, LANES - N_EXPERTS - N_GROUPS), F32)], axis=1).astype(F32)
    w_r_hi = w_r.astype(BF16)
    w_r_mid = (w_r - w_r_hi.astype(F32)).astype(BF16)
    wr = jnp.concatenate([w_r_hi, w_r_mid], axis=1)
    b_r = jnp.concatenate([moe_b_expert[layer], moe_b_group[layer],
                           jnp.zeros((LANES - N_EXPERTS - N_GROUPS,), F32)]).astype(F32)
    br = jnp.broadcast_to(b_r[None, :], (SUBLANES, LANES))
    x2, r = _attn(x1, kv, xa_w_q[layer].astype(BF16), xa_w_o[layer].astype(BF16), ln2, wr, br, nb, t, tm_row)

    e1 = r[:, 0].astype(jnp.int32)
    e2 = r[:, 1].astype(jnp.int32)
    block_e, block_nv, slot_tok, pos = _route_slots(e1, e2, MOE_TB, tm_row)
    y_slots = _moe(x2, block_e, block_nv, slot_tok, moe_w_gate[layer].astype(F32),
                   moe_w_up[layer].astype(F32), moe_w_down[layer].astype(F32))

    ln3 = _pad_rows(jnp.stack([ln3_g[layer], ln3_b[layer]], axis=0).astype(F32))
    y_first, y_rest = _combine(y_slots, pos, x2, r, ln3, tm_row, nb_first * t)
    return y_first.reshape(nb_first, t, D_MODEL), y_rest.reshape(nb - nb_first, t, D_MODEL)


def kernel(x_prompt, x_sample, mem_prompt, mem_sample, w_in, hgrn_lb, hgrn_norm_w, gdn_conv_w, gdn_a_log, gdn_dt_bias, gdn_norm_w, w_out, ln1_g, ln1_b, xa_w_q, xa_w_kv, xa_w_o, ln2_g, ln2_b, moe_w_group, moe_b_group, moe_w_expert, moe_b_expert, moe_w_gate, moe_w_up, moe_w_down, ln3_g, ln3_b):
    assert x_prompt.shape[1] == x_sample.shape[1]
    mem = jnp.concatenate([mem_prompt, mem_sample], axis=0)
    return _layer(x_prompt, x_sample, mem, w_in, hgrn_lb, hgrn_norm_w, gdn_conv_w, gdn_a_log, gdn_dt_bias,
                  gdn_norm_w, w_out, ln1_g, ln1_b, xa_w_q, xa_w_kv, xa_w_o, ln2_g, ln2_b, moe_w_group, moe_b_group,
                  moe_w_expert, moe_b_expert, moe_w_gate, moe_w_up, moe_w_down, ln3_g, ln3_b)
```

```python
import functools
import math

import numpy as np
import jax
import jax.numpy as jnp
from jax import lax
from jax.experimental import pallas as pl
from jax.experimental.pallas import tpu as pltpu

F32 = jnp.float32
BF16 = jnp.bfloat16

D_MODEL = 2048
N_HEADS = 8
D_HEAD = 128
SEG = N_HEADS * D_HEAD
CONV_K = 5
XA_HEADS = 4
XA_HEAD_DIM = D_MODEL // XA_HEADS
N_GROUPS = 4
EXPERTS_PER_GROUP = 8
N_EXPERTS = N_GROUPS * EXPERTS_PER_GROUP
D_EXPERT = 512
DN_ALPHA = 2.0 ** 0.25
LN_EPS = 1e-5
NORM_EPS = 1e-6
LOG2E = math.log2(math.e)

LANES = 128
SUBLANES = 8
VMEM_LIMIT = 56 * 1024 * 1024

HG_CHUNK = 128
HG_TBK = 256
HG_HPG = 8
HG_SMALL_LEVELS = 3
GD_CHUNK = 128
GD_TBK = 256
GD_HPG = 8
MOE_TB = 256


def _cparams(sem):
    return pltpu.CompilerParams(dimension_semantics=sem, vmem_limit_bytes=VMEM_LIMIT)


def _dot(a, b):
    return jnp.dot(a, b, preferred_element_type=F32)


def _dot_nt(a, b):
    return lax.dot_general(a, b, (((1,), (1,)), ((), ())), preferred_element_type=F32)


def _dot_tn(a, b):
    return lax.dot_general(a, b, (((0,), (0,)), ((), ())), preferred_element_type=F32)


def _split2(x):
    hi = x.astype(BF16)
    mid = (x - hi.astype(F32)).astype(BF16)
    return hi, mid


def _sigmoid(x):
    return 1.0 / (1.0 + jnp.exp(-x))


def _layernorm_rows(y, g, b):
    mu = jnp.mean(y, axis=-1, keepdims=True)
    yc = y - mu
    var = jnp.mean(yc * yc, axis=-1, keepdims=True)
    return yc * lax.rsqrt(var + LN_EPS) * g + b


def _mm_kernel(x_ref, w_ref, o_ref):
    o_ref[...] = _dot(x_ref[...], w_ref[...]).astype(o_ref.dtype)


def _matmul(x, w, tm, tn, out_dtype):
    m, k = x.shape
    n = w.shape[1]
    return pl.pallas_call(
        _mm_kernel,
        grid=(m // tm, n // tn),
        in_specs=[pl.BlockSpec((tm, k), lambda i, j: (i, 0)),
                  pl.BlockSpec((k, tn), lambda i, j: (0, j))],
        out_specs=pl.BlockSpec((tm, tn), lambda i, j: (i, j)),
        out_shape=jax.ShapeDtypeStruct((m, n), out_dtype),
        compiler_params=_cparams(("parallel", "arbitrary")),
        name="dense_matmul",
    )(x, w)


def _mm2_kernel(xa_ref, xb_ref, w_ref, ws_ref, o_ref, os_ref, *, na):
    i = pl.program_id(0)
    j = pl.program_id(1)

    def project(x_ref):
        xb16 = x_ref[...].astype(BF16)
        o_ref[...] = _dot(xb16, w_ref[...])

        @pl.when(j == 0)
        def _():
            os_ref[...] = _dot(xb16, ws_ref[...])

    @pl.when(i < na)
    def _():
        project(xa_ref)

    @pl.when(i >= na)
    def _():
        project(xb_ref)


def _in_proj(xa, xb, w, w_small, tm, tn):
    k = xa.shape[1]
    n = w.shape[1]
    ns = w_small.shape[1]
    na = xa.shape[0] // tm
    nt = na + xb.shape[0] // tm
    return pl.pallas_call(
        functools.partial(_mm2_kernel, na=na),
        grid=(nt, n // tn),
        in_specs=[pl.BlockSpec((tm, k), lambda i, j: (jnp.minimum(i, na - 1), 0)),
                  pl.BlockSpec((tm, k), lambda i, j: (jnp.maximum(i - na, 0), 0)),
                  pl.BlockSpec((k, tn), lambda i, j: (0, j)),
                  pl.BlockSpec((k, ns), lambda i, j: (0, 0))],
        out_specs=[pl.BlockSpec((tm, tn), lambda i, j: (i, j)),
                   pl.BlockSpec((tm, ns), lambda i, j: (i, 0))],
        out_shape=[jax.ShapeDtypeStruct((nt * tm, n), F32), jax.ShapeDtypeStruct((nt * tm, ns), F32)],
        compiler_params=_cparams(("parallel", "arbitrary")),
        name="in_proj_matmul",
    )(xa, xb, w, w_small)


def _hgrn_constants(c):
    nl = int(math.log2(c))
    t = np.arange(c)
    u = np.arange(c)
    blocks = []
    blocks.append((u[None, :] <= t[:, None]).astype(np.float32))
    masks, qrows = [], []
    for l in range(nl):
        cc = 1 << l
        m = (t // (2 * cc)) * (2 * cc) + cc
        upper = ((t >> l) & 1) == 1
        a = np.zeros((c, c), np.float32)
        for r in range(c):
            if upper[r]:
                a[r, m[r]:r + 1] = 1.0
            else:
                a[r, r + 1:m[r]] = 1.0
        if 1 <= l < HG_SMALL_LEVELS:
            blocks.append(a)
        same = (t[:, None] >> (l + 1)) == (t[None, :] >> (l + 1))
        masks.append((same & upper[:, None] & (~upper)[None, :]).astype(np.float32))
        qrows.append(np.broadcast_to(upper[:, None], (c, D_HEAD)).astype(np.float32))
    blocks.append(np.ones((16, c), np.float32))
    a_f = np.concatenate(blocks, axis=0)
    m_f = np.stack(masks)
    q_f = np.stack(qrows)
    a_b = np.concatenate([b[::-1, ::-1] for b in blocks], axis=0)
    m_b = m_f[:, ::-1, ::-1]
    q_b = q_f[:, ::-1, :]
    return (np.stack([a_f, a_b]), np.stack([m_f, m_b]).copy(), np.stack([q_f, q_b]).copy(), nl)


def _hgrn_kernel(q_ref, f_ref, v_ref, lb_ref, a_ref, m_ref, qr_ref, o_ref, st_ref, *, c, nck, nl, scale, hpg):
    d = pl.program_id(0)
    i = pl.program_id(3)

    @pl.when(i == 0)
    def _():
        st_ref[...] = jnp.zeros_like(st_ref)

    probs = [(j, g) for j in range(nck) for g in range(hpg)]

    def rows(j):
        jj = jnp.where(d == 0, j, nck - 1 - j)
        return pl.ds(pl.multiple_of(jj * c, c), c)

    def cols(g):
        return slice(g * D_HEAD, (g + 1) * D_HEAD)

    def eblk(k, g2, rows=c):
        e2 = _dot(a_ref[pl.ds(k * c, rows), :], g2)
        return e2[:, :D_HEAD] + e2[:, D_HEAD:]

    def boundary(gcum, l):
        cc = 1 << l
        pieces = []
        for p in range(c // (2 * cc)):
            m = p * 2 * cc + cc
            rowv = jnp.where(d == 0, gcum[m - 1:m, :], gcum[m:m + 1, :])
            pieces.append(jnp.broadcast_to(rowv, (2 * cc, D_HEAD)))
        return pieces[0] if len(pieces) == 1 else jnp.concatenate(pieces, axis=0)

    qs, kk, ff, vb, vv, g2, gcum, gtot = {}, {}, {}, {}, {}, {}, {}, {}
    for (j, g) in probs:
        lb = lb_ref[g, 0:1, :]
        one_m_lb = 1.0 - lb
        fr = f_ref[rows(j), cols(g)]
        qs[j, g] = q_ref[rows(j), cols(g)] * scale
        v = v_ref[rows(j), cols(g)]
        vv[j, g] = v
        vb[j, g] = v.astype(BF16)
        sig = _sigmoid(fr)
        ff[j, g] = lb + one_m_lb * sig
        kk[j, g] = one_m_lb * (1.0 - sig)
        gl = jnp.log(ff[j, g]) * LOG2E
        g_hi, g_mid = _split2(gl)
        g2[j, g] = jnp.concatenate([g_hi, g_mid], axis=1)
    for p in probs:
        gcum[p] = eblk(0, g2[p])
        gtot[p] = eblk(HG_SMALL_LEVELS, g2[p], rows=16)[0:1, :]

    s = {}
    for l in range(nl):
        if l == 0:
            wq = {p: (qs[p] * ff[p]).astype(BF16) for p in probs}
            wk = {p: kk[p].astype(BF16) for p in probs}
        else:
            if l < HG_SMALL_LEVELS:
                z = {p: jnp.exp2(eblk(l, g2[p])) for p in probs}
            else:
                sgn = 2.0 * qr_ref[l] - 1.0
                z = {p: jnp.exp2((gcum[p] - boundary(gcum[p], l)) * sgn) for p in probs}
            wq = {p: (qs[p] * z[p]).astype(BF16) for p in probs}
            wk = {p: (kk[p] * z[p]).astype(BF16) for p in probs}
        sd = {p: _dot_nt(wq[p], wk[p]) for p in probs}
        for p in probs:
            s[p] = m_ref[l] * sd[p] if l == 0 else s[p] + m_ref[l] * sd[p]

    o_in, qd, upd, dec = {}, {}, {}, {}
    for p in probs:
        dg = jnp.sum(qs[p] * kk[p], axis=-1, keepdims=True)
        o_in[p] = _dot(s[p].astype(BF16), vb[p]) + dg * vv[p]
        qd[p] = (qs[p] * jnp.exp2(gcum[p])).astype(BF16)
        kd = (kk[p] * jnp.exp2(gtot[p] - gcum[p])).astype(BF16)
        upd[p] = _dot_tn(vb[p], kd)
        dec[p] = jnp.exp2(gtot[p])

    st = [st_ref[g] for g in range(hpg)]
    for j in range(nck):
        for g in range(hpg):
            p = (j, g)
            o_ref[rows(j), cols(g)] = (o_in[p] + _dot_nt(qd[p], st[g].astype(BF16))).astype(o_ref.dtype)
            st[g] = st[g] * dec[p] + upd[p]
    for g in range(hpg):
        st_ref[g] = st[g]


def _hgrn2(proj, lb2, nb, t, tbk, hpg):
    c = HG_CHUNK
    n = nb * t
    nblk = t // tbk
    a_np, m_np, q_np, nl = _hgrn_constants(c)
    a_all = jnp.asarray(a_np, BF16)
    masks = jnp.asarray(m_np, F32)
    qrows = jnp.asarray(q_np, F32)
    lb8 = jnp.broadcast_to(lb2.reshape(2, N_HEADS, 1, D_HEAD), (2, N_HEADS, SUBLANES, D_HEAD))
    ngrp = N_HEADS // hpg
    wb = hpg * D_HEAD

    def row(d, b, i):
        return b * nblk + jnp.where(d == 0, i, nblk - 1 - i)

    kern = functools.partial(_hgrn_kernel, c=c, nck=tbk // c, nl=nl, scale=D_HEAD ** -0.5, hpg=hpg)
    na = a_np.shape[1]
    return pl.pallas_call(
        kern,
        grid=(2, nb, ngrp, nblk),
        in_specs=[
            pl.BlockSpec((tbk, wb), lambda d, b, h, i: (row(d, b, i), h)),
            pl.BlockSpec((tbk, wb), lambda d, b, h, i: (row(d, b, i), (1 + d) * ngrp + h)),
            pl.BlockSpec((tbk, wb), lambda d, b, h, i: (row(d, b, i), 3 * ngrp + h)),
            pl.BlockSpec((None, hpg, SUBLANES, D_HEAD), lambda d, b, h, i: (d, h, 0, 0)),
            pl.BlockSpec((None, na, c), lambda d, b, h, i: (d, 0, 0)),
            pl.BlockSpec((None, nl, c, c), lambda d, b, h, i: (d, 0, 0, 0)),
            pl.BlockSpec((None, nl, c, D_HEAD), lambda d, b, h, i: (d, 0, 0, 0)),
        ],
        out_specs=pl.BlockSpec((None, tbk, wb), lambda d, b, h, i: (d, row(d, b, i), h)),
        out_shape=jax.ShapeDtypeStruct((2, n, SEG), BF16),
        scratch_shapes=[pltpu.VMEM((hpg, D_HEAD, D_HEAD), F32)],
        compiler_params=_cparams(("parallel", "parallel", "parallel", "arbitrary")),
        name="hgrn2_bidir",
    )(proj, proj, proj, lb8, a_all, masks, qrows)


def _gdn_prep_kernel(prev_ref, cur_ref, nxt_ref, w_ref, o_ref, *, tbk, scale):
    i = pl.program_id(1)
    s = pl.program_id(2)
    nblk = pl.num_programs(1)
    prev = jnp.where(i == 0, 0.0, prev_ref[...])
    nxt = jnp.where(i == nblk - 1, 0.0, nxt_ref[...])
    full = jnp.concatenate([prev, cur_ref[...], nxt], axis=0)
    rows = tbk + 2 * SUBLANES
    acc = jnp.zeros((tbk, SEG), F32)
    for j in range(CONV_K):
        shift = (CONV_K // 2 - j) % rows
        sh = full if shift == 0 else pltpu.roll(full, shift, axis=0)
        acc = acc + sh[SUBLANES:SUBLANES + tbk, :] * w_ref[j:j + 1, :]
    y = acc * _sigmoid(acc)
    fac_all = jnp.where(s == 0, scale, 1.0)
    for h in range(N_HEADS):
        ys = y[:, h * D_HEAD:(h + 1) * D_HEAD]
        ss = jnp.sum(ys * ys, axis=-1, keepdims=True)
        fac = jnp.where(s < 2, lax.rsqrt(ss + NORM_EPS), 1.0) * fac_all
        o_ref[:, h * D_HEAD:(h + 1) * D_HEAD] = ys * fac


def _gdn_prep(proj, conv_w, nb, t, tbk):
    n = nb * t
    nblk = t // tbk
    hb = tbk // SUBLANES
    w8 = jnp.concatenate([conv_w, jnp.zeros((SUBLANES - CONV_K, 3 * SEG), F32)], axis=0)
    last = n // SUBLANES - 1
    kern = functools.partial(_gdn_prep_kernel, tbk=tbk, scale=D_HEAD ** -0.5)
    return pl.pallas_call(
        kern,
        grid=(nb, nblk, 3),
        in_specs=[
            pl.BlockSpec((SUBLANES, SEG), lambda b, i, s: (jnp.maximum((b * nblk + i) * hb - 1, 0), 5 + s)),
            pl.BlockSpec((tbk, SEG), lambda b, i, s: (b * nblk + i, 5 + s)),
            pl.BlockSpec((SUBLANES, SEG), lambda b, i, s: (jnp.minimum((b * nblk + i + 1) * hb, last), 5 + s)),
            pl.BlockSpec((SUBLANES, SEG), lambda b, i, s: (0, s)),
        ],
        out_specs=pl.BlockSpec((None, tbk, SEG), lambda b, i, s: (s, b * nblk + i, 0)),
        out_shape=jax.ShapeDtypeStruct((3, n, SEG), F32),
        compiler_params=_cparams(("parallel", "parallel", "arbitrary")),
        name="gdn_conv_prep",
    )(proj, proj, proj, w8)


def _gdn_constants(c):
    nl = int(math.log2(c))
    t = np.arange(c)
    incl = (t[None, :] <= t[:, None]).astype(np.float32)
    strict = (t[None, :] < t[:, None]).astype(np.float32)
    after = (t[:, None] > t[None, :]).astype(np.float32)
    rhs_f = np.concatenate([np.ones((c, LANES), np.float32), after], axis=1)
    lvl = []
    for l in range(nl):
        upper = ((t >> l) & 1) == 1
        same = (t[:, None] >> (l + 1)) == (t[None, :] >> (l + 1))
        lvl.append((same & upper[:, None] & (~upper)[None, :]).astype(np.float32))
    lvl = np.stack(lvl)
    mk_f = np.concatenate([incl[None], strict[None], lvl], axis=0)
    mk_b = mk_f[:, ::-1, ::-1]
    rhs_b = np.concatenate([np.ones((c, LANES), np.float32), after[::-1, ::-1]], axis=1)
    return np.stack([mk_f, mk_b]).copy(), np.stack([rhs_f, rhs_b]).copy(), nl


def _gdn_kernel(coef_ref, q_ref, k_ref, v_ref, arow_ref, ps_ref, mk_ref, rhs_ref, o_ref, st_ref, *, c, nck, nl, hpg):
    d = pl.program_id(0)
    hg = pl.program_id(2)
    i = pl.program_id(3)

    @pl.when(i == 0)
    def _():
        st_ref[...] = jnp.zeros_like(st_ref)

    lane = lax.broadcasted_iota(jnp.int32, (c, LANES), 1)
    incl = mk_ref[0]
    strict = mk_ref[1]
    eye = incl - strict
    incl_b = incl.astype(BF16)
    rhs = rhs_ref[...]
    probs = [(j, g) for j in range(nck) for g in range(hpg)]

    def rows(j):
        jj = jnp.where(d == 0, j, nck - 1 - j)
        return pl.ds(pl.multiple_of(jj * c, c), c)

    def cols(g):
        return slice(g * D_HEAD, (g + 1) * D_HEAD)

    qn, kn, vv, beta, gcum, dec, etot = {}, {}, {}, {}, {}, {}, {}
    for (j, g) in probs:
        h = hg * hpg + g
        neg_ea = coef_ref[0, d * N_HEADS + h]
        dtb = coef_ref[1, d * N_HEADS + h]
        a_blk = arow_ref[pl.ds(h, 1), :]
        a_row = jnp.where(d == 0, a_blk[:, j * c:(j + 1) * c], a_blk[:, (nck - 1 - j) * c:(nck - j) * c])
        z = a_row + dtb
        g_row = neg_ea * (jnp.maximum(z, 0.0) + jnp.log(1.0 + jnp.exp(-jnp.abs(z))))
        beta_lane = 2 * N_HEADS + d * N_HEADS + h
        b_col = jnp.sum(jnp.where(lane == beta_lane, ps_ref[rows(j), :], 0.0), axis=-1, keepdims=True)
        beta[j, g] = _sigmoid(b_col)
        g_hi, g_mid = _split2(g_row)
        out = _dot(jnp.concatenate([incl_b * g_hi, incl_b * g_mid], axis=0), rhs)
        out = out[:c] + out[c:]
        gcum[j, g] = out[:, :LANES]
        dmat = out[:, LANES:]
        dec[j, g] = jnp.exp2(jnp.minimum(dmat, 0.0)) * incl
        qn[j, g] = q_ref[rows(j), cols(g)]
        kn[j, g] = k_ref[rows(j), cols(g)]
        vv[j, g] = v_ref[rows(j), cols(g)]

    kb, knb, a_dec, tinv = {}, {}, {}, {}
    for p in probs:
        kb[p] = kn[p] * beta[p]
        knb[p] = kn[p].astype(BF16)
        a_dec[p] = _dot_nt(kb[p].astype(BF16), knb[p]) * dec[p]
        tinv[p] = eye - a_dec[p] * mk_ref[2]
    a_b = {p: a_dec[p].astype(BF16) for p in probs}
    for l in range(1, nl):
        ml_b = mk_ref[2 + l].astype(BF16)
        tb = {p: tinv[p].astype(BF16) for p in probs}
        ta = {p: _dot(tb[p], a_b[p] * ml_b).astype(BF16) for p in probs}
        for p in probs:
            tinv[p] = tinv[p] - _dot(ta[p], tb[p])

    u, w, qk, qe, ke = {}, {}, {}, {}, {}
    for p in probs:
        gtot = jnp.where(d == 0, gcum[p][c - 1:c, :], gcum[p][0:1, :])
        eg = jnp.exp2(gcum[p])
        etot[p] = jnp.exp2(gtot)
        uw = _dot(tinv[p].astype(BF16), jnp.concatenate([vv[p] * beta[p], kb[p] * eg], axis=1).astype(BF16))
        u[p] = uw[:, :D_HEAD]
        w[p] = uw[:, D_HEAD:].astype(BF16)
        qk[p] = (_dot_nt(qn[p].astype(BF16), knb[p]) * dec[p]).astype(BF16)
        qe[p] = (qn[p] * eg).astype(BF16)
        ke[p] = (kn[p] * jnp.exp2(gtot - gcum[p])).astype(BF16)

    st = [st_ref[g] for g in range(hpg)]
    for j in range(nck):
        stb = [st[g].astype(BF16) for g in range(hpg)]
        ws = [_dot(w[j, g], stb[g]) for g in range(hpg)]
        qs = [_dot(qe[j, g], stb[g]) for g in range(hpg)]
        vnb = [(u[j, g] - ws[g]).astype(BF16) for g in range(hpg)]
        for g in range(hpg):
            o_ref[rows(j), cols(g)] = (qs[g] + _dot(qk[j, g], vnb[g])).astype(o_ref.dtype)
        for g in range(hpg):
            st[g] = st[g] * etot[j, g] + _dot_tn(ke[j, g], vnb[g])
    for g in range(hpg):
        st_ref[g] = st[g]


def _gdn(qkv, ps, ps_t, coef, nb, t, tbk, hpg):
    c = GD_CHUNK
    n = nb * t
    nblk = t // tbk
    mk_np, rhs_np, nl = _gdn_constants(c)
    mk = jnp.asarray(mk_np, F32)
    rhs = jnp.asarray(rhs_np, BF16)
    wb = hpg * D_HEAD

    def row(d, b, i):
        return b * nblk + jnp.where(d == 0, i, nblk - 1 - i)

    kern = functools.partial(_gdn_kernel, c=c, nck=tbk // c, nl=nl, hpg=hpg)
    grid_spec = pltpu.PrefetchScalarGridSpec(
        num_scalar_prefetch=0,
        grid=(2, nb, N_HEADS // hpg, nblk),
        in_specs=[
            pl.BlockSpec(memory_space=pltpu.SMEM),
            pl.BlockSpec((None, tbk, wb), lambda d, b, h, i: (0, row(d, b, i), h)),
            pl.BlockSpec((None, tbk, wb), lambda d, b, h, i: (1, row(d, b, i), h)),
            pl.BlockSpec((None, tbk, wb), lambda d, b, h, i: (2, row(d, b, i), h)),
            pl.BlockSpec((SUBLANES, tbk), lambda d, b, h, i: (d, row(d, b, i))),
            pl.BlockSpec((tbk, LANES), lambda d, b, h, i: (row(d, b, i), 0)),
            pl.BlockSpec((None, 2 + nl, c, c), lambda d, b, h, i: (d, 0, 0, 0)),
            pl.BlockSpec((None, c, c + LANES), lambda d, b, h, i: (d, 0, 0)),
        ],
        out_specs=pl.BlockSpec((None, tbk, wb), lambda d, b, h, i: (d, row(d, b, i), h)),
        scratch_shapes=[pltpu.VMEM((hpg, D_HEAD, D_HEAD), F32)],
    )
    return pl.pallas_call(
        kern,
        grid_spec=grid_spec,
        out_shape=jax.ShapeDtypeStruct((2, n, SEG), BF16),
        compiler_params=_cparams(("parallel", "parallel", "parallel", "arbitrary")),
        name="gdn_bidir",
    )(coef, qkv, qkv, qkv, ps_t, ps, mk, rhs)


def _mixout_kernel(oh_ref, og_ref, hg_ref, gz_ref, xa_ref, xb_ref, w_ref, nw_ref, ln_ref, o_ref, lhs_ref, *, na):
    i = pl.program_id(0)

    def gated(o2_ref, gate_ref, wrow, col0):
        for h in range(N_HEADS):
            sl = slice(h * D_HEAD, (h + 1) * D_HEAD)
            o = o2_ref[0, :, sl].astype(F32) + o2_ref[1, :, sl].astype(F32)
            ms = jnp.mean(o * o, axis=-1, keepdims=True)
            gt = gate_ref[:, sl]
            y = o * lax.rsqrt(ms + NORM_EPS) * wrow * (gt * _sigmoid(gt))
            lhs_ref[:, col0 + h * D_HEAD:col0 + (h + 1) * D_HEAD] = y.astype(BF16)

    gated(oh_ref, hg_ref, nw_ref[0:1, :], 0)
    gated(og_ref, gz_ref, nw_ref[1:2, :], SEG)
    hmix = _dot(lhs_ref[...], w_ref[...])
    x = jnp.where(i < na, xa_ref[...], xb_ref[...])
    y = DN_ALPHA * x + hmix
    o_ref[...] = _layernorm_rows(y, ln_ref[0:1, :], ln_ref[1:2, :])


def _mixout(oh, og, proj, xa, xb, w_out_b, nw, ln, tm):
    na = xa.shape[0] // tm
    n = xa.shape[0] + xb.shape[0]
    return pl.pallas_call(
        functools.partial(_mixout_kernel, na=na),
        grid=(n // tm,),
        in_specs=[
            pl.BlockSpec((2, tm, SEG), lambda i: (0, i, 0)),
            pl.BlockSpec((2, tm, SEG), lambda i: (0, i, 0)),
            pl.BlockSpec((tm, SEG), lambda i: (i, 4)),
            pl.BlockSpec((tm, SEG), lambda i: (i, 8)),
            pl.BlockSpec((tm, D_MODEL), lambda i: (jnp.minimum(i, na - 1), 0)),
            pl.BlockSpec((tm, D_MODEL), lambda i: (jnp.maximum(i - na, 0), 0)),
            pl.BlockSpec((2 * SEG, D_MODEL), lambda i: (0, 0)),
            pl.BlockSpec((SUBLANES, D_HEAD), lambda i: (0, 0)),
            pl.BlockSpec((SUBLANES, D_MODEL), lambda i: (0, 0)),
        ],
        out_specs=pl.BlockSpec((tm, D_MODEL), lambda i: (i, 0)),
        out_shape=jax.ShapeDtypeStruct((n, D_MODEL), F32),
        scratch_shapes=[pltpu.VMEM((tm, 2 * SEG), BF16)],
        compiler_params=_cparams(("parallel",)),
        name="mixer_out_ln1",
    )(oh, og, proj, proj, xa, xb, w_out_b, nw, ln)


def _attn_kernel(x_ref, kv_ref, wq_ref, wo_ref, ln_ref, wr_ref, br_ref, o_ref, r_ref, att_ref):
    x1 = x_ref[...]
    q = _dot(x1.astype(BF16), wq_ref[...])
    sc = XA_HEAD_DIM ** -0.5
    for h in range(XA_HEADS):
        sl = slice(h * XA_HEAD_DIM, (h + 1) * XA_HEAD_DIM)
        kh = kv_ref[:, sl]
        vh = kv_ref[:, D_MODEL + h * XA_HEAD_DIM:D_MODEL + (h + 1) * XA_HEAD_DIM]
        s = _dot_nt(q[:, sl].astype(BF16), kh) * sc
        s = s - jnp.max(s, axis=-1, keepdims=True)
        p = jnp.exp(s)
        p = p / jnp.sum(p, axis=-1, keepdims=True)
        att_ref[:, sl] = _dot(p.astype(BF16), vh).astype(BF16)
    hx = _dot(att_ref[...], wo_ref[...])
    x2 = _layernorm_rows(DN_ALPHA * x1 + hx, ln_ref[0:1, :], ln_ref[1:2, :])
    o_ref[...] = x2

    x_hi, x_mid = _split2(x2)
    lg2 = _dot(x_hi, wr_ref[...])
    lg = lg2[:, :LANES] + lg2[:, LANES:] + _dot(x_mid, wr_ref[:, :LANES]) + br_ref[0:1, :]
    tm = lg.shape[0]
    lane_i = lax.broadcasted_iota(jnp.int32, (tm, LANES), 1)
    lane = lane_i.astype(F32)
    grp_of_lane = (lane_i >> 3).astype(F32)
    neg = jnp.float32(-1e30)
    big = jnp.float32(LANES)
    is_g = (lane_i >= N_EXPERTS) & (lane_i < N_EXPERTS + N_GROUPS)
    gl = jnp.where(is_g, lg, neg)
    gmax = jnp.max(gl, axis=-1, keepdims=True)
    gsel = jnp.min(jnp.where(gl == gmax, lane, big), axis=-1, keepdims=True) - N_EXPERTS
    p_group = 1.0 / jnp.sum(jnp.where(is_g, jnp.exp(gl - gmax), 0.0), axis=-1, keepdims=True)
    in_grp = (lane_i < N_EXPERTS) & (grp_of_lane == gsel)
    el = jnp.where(in_grp, lg, neg)
    m1 = jnp.max(el, axis=-1, keepdims=True)
    i1 = jnp.min(jnp.where(el == m1, lane, big), axis=-1, keepdims=True)
    el2 = jnp.where(lane == i1, neg, el)
    m2 = jnp.max(el2, axis=-1, keepdims=True)
    i2 = jnp.min(jnp.where(el2 == m2, lane, big), axis=-1, keepdims=True)
    e2 = jnp.exp(m2 - m1)
    g1 = p_group / (1.0 + e2)
    g2 = p_group * e2 / (1.0 + e2)
    r = jnp.where(lane_i == 0, i1, 0.0)
    r = jnp.where(lane_i == 1, i2, r)
    r = jnp.where(lane_i == 2, g1, r)
    r = jnp.where(lane_i == 3, g2, r)
    r_ref[...] = r


def _attn(x1, kv, wq_b, wo_b, ln, wr, br, nb, t, tm):
    n = nb * t
    nblk = t // tm
    n_mem = kv.shape[0] // nb
    return pl.pallas_call(
        _attn_kernel,
        grid=(nb, nblk),
        in_specs=[
            pl.BlockSpec((tm, D_MODEL), lambda b, i: (b * nblk + i, 0)),
            pl.BlockSpec((n_mem, 2 * D_MODEL), lambda b, i: (b, 0)),
            pl.BlockSpec((D_MODEL, D_MODEL), lambda b, i: (0, 0)),
            pl.BlockSpec((D_MODEL, D_MODEL), lambda b, i: (0, 0)),
            pl.BlockSpec((SUBLANES, D_MODEL), lambda b, i: (0, 0)),
            pl.BlockSpec((D_MODEL, 2 * LANES), lambda b, i: (0, 0)),
            pl.BlockSpec((SUBLANES, LANES), lambda b, i: (0, 0)),
        ],
        out_specs=[pl.BlockSpec((tm, D_MODEL), lambda b, i: (b * nblk + i, 0)),
                   pl.BlockSpec((tm, LANES), lambda b, i: (b * nblk + i, 0))],
        out_shape=[jax.ShapeDtypeStruct((n, D_MODEL), F32), jax.ShapeDtypeStruct((n, LANES), F32)],
        scratch_shapes=[pltpu.VMEM((tm, D_MODEL), BF16)],
        compiler_params=_cparams(("parallel", "arbitrary")),
        name="xattn_ln2_router",
    )(x1, kv, wq_b, wo_b, ln, wr, br)


DMA_UNROLL = 8


def _issue_row_gather(idx_ref, src_hbm, dst, sem, n_rows):
    def body(k, carry):
        for u in range(DMA_UNROLL):
            r = k * DMA_UNROLL + u
            pltpu.make_async_copy(src_hbm.at[pl.ds(idx_ref[0, r], 1)], dst.at[pl.ds(r, 1)], sem).start(priority=u % 2)
        return carry

    lax.fori_loop(0, n_rows // DMA_UNROLL, body, 0)


def _issue_row_gather_inline(idx_ref, src_hbm, dst, sem, n_rows):
    for r in range(n_rows):
        pltpu.make_async_copy(src_hbm.at[pl.ds(idx_ref[0, r], 1)], dst.at[pl.ds(r, 1)], sem).start(priority=r % 2)


def _wait_row_gather(src_hbm, dst, sem, n_rows):
    pltpu.make_async_copy(src_hbm.at[pl.ds(0, n_rows)], dst, sem).wait()


def _moe_kernel(be_ref, nv_ref, tok_ref, tokn_ref, x_hbm, wg_ref, wu_ref, wd_ref, o_ref, xbuf, sem,
                wgb, wub, wdb, *, tb):
    b = pl.program_id(0)
    nb = pl.num_programs(0)
    slot = b % 2

    @pl.when((b == 0) | (be_ref[b] != be_ref[jnp.maximum(b - 1, 0)]))
    def _():
        wgb[...] = wg_ref[...].astype(BF16)
        wub[...] = wu_ref[...].astype(BF16)
        wdb[...] = wd_ref[...].astype(BF16)

    @pl.when((b == 0) & (nv_ref[0] > 0))
    def _():
        _issue_row_gather(tok_ref, x_hbm, xbuf.at[0], sem.at[0], tb)

    has_next = (b + 1 < nb) & (nv_ref[jnp.minimum(b + 1, nb - 1)] > 0)

    def compute():
        xb = xbuf[slot].astype(BF16)
        h1 = _dot(xb, wgb[...])
        h2 = _dot(xb, wub[...])
        hdn = (h1 * _sigmoid(h1) * h2).astype(BF16)
        o_ref[...] = _dot(hdn, wdb[...])

    @pl.when((nv_ref[b] > 0) & has_next)
    def _():
        _wait_row_gather(x_hbm, xbuf.at[slot], sem.at[slot], tb)
        _issue_row_gather_inline(tokn_ref, x_hbm, xbuf.at[1 - slot], sem.at[1 - slot], tb)
        compute()

    @pl.when((nv_ref[b] > 0) & jnp.logical_not(has_next))
    def _():
        _wait_row_gather(x_hbm, xbuf.at[slot], sem.at[slot], tb)
        compute()

    @pl.when((nv_ref[b] == 0) & has_next)
    def _():
        _issue_row_gather(tokn_ref, x_hbm, xbuf.at[1 - slot], sem.at[1 - slot], tb)

    @pl.when(nv_ref[b] == 0)
    def _():
        o_ref[...] = jnp.zeros_like(o_ref)


def _moe(x2, block_e, block_nv, slot_tok, wg_b, wu_b, wd_b):
    tb = MOE_TB
    cap = slot_tok.shape[0]
    n_blocks = cap // tb
    tok3 = slot_tok.reshape(n_blocks, 1, tb)
    grid_spec = pltpu.PrefetchScalarGridSpec(
        num_scalar_prefetch=2,
        grid=(n_blocks,),
        in_specs=[
            pl.BlockSpec((None, 1, tb), lambda b, be, nv: (b, 0, 0), memory_space=pltpu.SMEM),
            pl.BlockSpec((None, 1, tb), lambda b, be, nv: (jnp.minimum(b + 1, n_blocks - 1), 0, 0),
                         memory_space=pltpu.SMEM),
            pl.BlockSpec(memory_space=pl.ANY),
            pl.BlockSpec((None, D_MODEL, D_EXPERT), lambda b, be, nv: (be[b], 0, 0)),
            pl.BlockSpec((None, D_MODEL, D_EXPERT), lambda b, be, nv: (be[b], 0, 0)),
            pl.BlockSpec((None, D_EXPERT, D_MODEL), lambda b, be, nv: (be[b], 0, 0)),
        ],
        out_specs=pl.BlockSpec((tb, D_MODEL), lambda b, be, nv: (b, 0)),
        scratch_shapes=[pltpu.VMEM((2, tb, D_MODEL), F32), pltpu.SemaphoreType.DMA((2,)),
                        pltpu.VMEM((D_MODEL, D_EXPERT), BF16), pltpu.VMEM((D_MODEL, D_EXPERT), BF16),
                        pltpu.VMEM((D_EXPERT, D_MODEL), BF16)],
    )
    return pl.pallas_call(
        functools.partial(_moe_kernel, tb=tb),
        grid_spec=grid_spec,
        out_shape=jax.ShapeDtypeStruct((cap, D_MODEL), F32),
        compiler_params=_cparams(("arbitrary",)),
        name="moe_grouped_mlp",
    )(block_e, block_nv, tok3, tok3, x2, wg_b, wu_b, wd_b)


def _combine_kernel(pos_ref, posn_ref, y_hbm, x_ref, r_ref, ln_ref, op_ref, os_ref, ybuf, sem, *, tm, npt):
    i = pl.program_id(0)
    nt = pl.num_programs(0)
    slot = i % 2

    @pl.when(i == 0)
    def _():
        _issue_row_gather(pos_ref, y_hbm, ybuf.at[0], sem.at[0], 2 * tm)

    def finish():
        r = r_ref[...]
        y = r[:, 2:3] * ybuf[slot, 0:tm, :] + r[:, 3:4] * ybuf[slot, tm:2 * tm, :]
        res = _layernorm_rows(DN_ALPHA * x_ref[...] + y, ln_ref[0:1, :], ln_ref[1:2, :])

        @pl.when(i < npt)
        def _():
            op_ref[...] = res

        @pl.when(i >= npt)
        def _():
            os_ref[...] = res

    @pl.when(i + 1 < nt)
    def _():
        _wait_row_gather(y_hbm, ybuf.at[slot], sem.at[slot], 2 * tm)
        _issue_row_gather_inline(posn_ref, y_hbm, ybuf.at[1 - slot], sem.at[1 - slot], 2 * tm)
        finish()

    @pl.when(i + 1 >= nt)
    def _():
        _wait_row_gather(y_hbm, ybuf.at[slot], sem.at[slot], 2 * tm)
        finish()


def _combine(y_slots, pos, x2, r, ln, tm, n_first):
    n = x2.shape[0]
    nt = n // tm
    npt = n_first // tm
    grid_spec = pltpu.PrefetchScalarGridSpec(
        num_scalar_prefetch=0,
        grid=(nt,),
        in_specs=[
            pl.BlockSpec((None, 1, 2 * tm), lambda i: (i, 0, 0), memory_space=pltpu.SMEM),
            pl.BlockSpec((None, 1, 2 * tm), lambda i: (jnp.minimum(i + 1, nt - 1), 0, 0), memory_space=pltpu.SMEM),
            pl.BlockSpec(memory_space=pl.ANY),
            pl.BlockSpec((tm, D_MODEL), lambda i: (i, 0)),
            pl.BlockSpec((tm, LANES), lambda i: (i, 0)),
            pl.BlockSpec((SUBLANES, D_MODEL), lambda i: (0, 0)),
        ],
        out_specs=[pl.BlockSpec((tm, D_MODEL), lambda i: (jnp.minimum(i, npt - 1), 0)),
                   pl.BlockSpec((tm, D_MODEL), lambda i: (jnp.maximum(i - npt, 0), 0))],
        scratch_shapes=[pltpu.VMEM((2, 2 * tm, D_MODEL), F32), pltpu.SemaphoreType.DMA((2,))],
    )
    return pl.pallas_call(
        functools.partial(_combine_kernel, tm=tm, npt=npt),
        grid_spec=grid_spec,
        out_shape=[jax.ShapeDtypeStruct((n_first, D_MODEL), F32), jax.ShapeDtypeStruct((n - n_first, D_MODEL), F32)],
        compiler_params=_cparams(("arbitrary",)),
        name="moe_combine_ln3",
    )(pos, pos, y_slots, x2, r, ln)


def _route_slots(e1, e2, tb, tm):
    n = e1.shape[0]
    flat_e = jnp.stack([e1, e2], axis=1).reshape(-1)
    n_assign = 2 * n
    n_blocks = -(-n_assign // tb) + N_EXPERTS
    cap = n_blocks * tb
    experts = jnp.arange(N_EXPERTS, dtype=jnp.int32)
    counts = jnp.sum((flat_e[:, None] == experts[None, :]).astype(jnp.int32), axis=0)
    padded = (counts + tb - 1) // tb * tb
    pad_end = jnp.cumsum(padded)
    need_end = jnp.cumsum(padded - counts)
    fill = jnp.arange(cap - n_assign, dtype=jnp.int32)
    fill_e = jnp.sum((need_end[None, :] <= fill[:, None]).astype(jnp.int32), axis=1)
    keys = jnp.concatenate([flat_e, fill_e])
    a_id = jnp.concatenate([jnp.arange(n_assign, dtype=jnp.int32),
                            n_assign + jnp.arange(cap - n_assign, dtype=jnp.int32)])
    _, slot_a = lax.sort((keys, a_id), num_keys=1, is_stable=True)
    slot_tok = jnp.where(slot_a < n_assign, slot_a // 2, 0).astype(jnp.int32)
    _, slot_of = lax.sort((slot_a, jnp.arange(cap, dtype=jnp.int32)), num_keys=1)
    slot_of = slot_of[:n_assign]
    starts = jnp.arange(n_blocks, dtype=jnp.int32) * tb
    block_e = jnp.minimum(jnp.sum((pad_end[None, :] <= starts[:, None]).astype(jnp.int32), axis=1),
                          N_EXPERTS - 1).astype(jnp.int32)
    block_nv = jnp.sum((slot_a.reshape(n_blocks, tb) < n_assign).astype(jnp.int32), axis=1)
    pos = slot_of.reshape(n // tm, tm, 2).transpose(0, 2, 1).reshape(n // tm, 1, 2 * tm)
    return block_e, block_nv, slot_tok, pos


def _pad_rows(a, rows=SUBLANES):
    return jnp.concatenate([a, jnp.zeros((rows - a.shape[0],) + a.shape[1:], a.dtype)], axis=0)


def _layer(x_first, x_rest, mem, w_in, hgrn_lb, hgrn_norm_w, gdn_conv_w, gdn_a_log, gdn_dt_bias, gdn_norm_w, w_out,
           ln1_g, ln1_b, xa_w_q, xa_w_kv, xa_w_o, ln2_g, ln2_b, moe_w_group, moe_b_group,
           moe_w_expert, moe_b_expert, moe_w_gate, moe_w_up, moe_w_down, ln3_g, ln3_b,
           *, tm_mm=1024, tbk=512, tm_row=256):
    nb_first, t, _ = x_first.shape
    nb = nb_first + x_rest.shape[0]
    xa = x_first.reshape(nb_first * t, D_MODEL)
    xb = x_rest.reshape((nb - nb_first) * t, D_MODEL)
    layer = 0

    w_in_b = w_in[layer].astype(BF16)
    n_main = 9 * SEG
    w_small = jnp.concatenate([w_in_b[:, n_main:], jnp.zeros((D_MODEL, LANES - 4 * N_HEADS), BF16)], axis=1)
    proj, ps = _in_proj(xa, xb, w_in_b[:, :n_main], w_small, tm_mm, SEG)

    lb2 = jnp.cumsum(jax.nn.softmax(hgrn_lb.astype(F32), axis=0), axis=0)[layer]
    oh = _hgrn2(proj, lb2, nb, t, HG_TBK, HG_HPG)

    qkv = _gdn_prep(proj, gdn_conv_w[layer, :, 0, :].astype(F32), nb, t, tbk)
    coef = jnp.stack([-jnp.exp(gdn_a_log[layer].astype(F32)).reshape(-1) * LOG2E,
                      gdn_dt_bias[layer].astype(F32).reshape(-1)], axis=0)
    og = _gdn(qkv, ps, ps.T, coef, nb, t, GD_TBK, GD_HPG)

    nw = _pad_rows(jnp.stack([hgrn_norm_w[layer], gdn_norm_w[layer]], axis=0).astype(F32))
    ln1 = _pad_rows(jnp.stack([ln1_g[layer], ln1_b[layer]], axis=0).astype(F32))
    x1 = _mixout(oh, og, proj, xa, xb, w_out[layer].astype(BF16), nw, ln1, tm_row)

    n_mem = mem.shape[1]
    kv = _matmul(mem.reshape(nb * n_mem, D_MODEL).astype(BF16), xa_w_kv[layer].astype(BF16),
                 n_mem, 1024, BF16)
    ln2 = _pad_rows(jnp.stack([ln2_g[layer], ln2_b[layer]], axis=0).astype(F32))
    w_r = jnp.concatenate([moe_w_expert[layer], moe_w_group[layer],
                           jnp.zeros((D_MODEL, LANES - N_EXPERTS - N_GROUPS), F32)], axis=1).astype(F32)
    w_r_hi = w_r.astype(BF16)
    w_r_mid = (w_r - w_r_hi.astype(F32)).astype(BF16)
    wr = jnp.concatenate([w_r_hi, w_r_mid], axis=1)
    b_r = jnp.concatenate([moe_b_expert[layer], moe_b_group[layer],
                           jnp.zeros((LANES - N_EXPERTS - N_GROUPS,), F32)]).astype(F32)
    br = jnp.broadcast_to(b_r[None, :], (SUBLANES, LANES))
    x2, r = _attn(x1, kv, xa_w_q[layer].astype(BF16), xa_w_o[layer].astype(BF16), ln2, wr, br, nb, t, tm_row)

    e1 = r[:, 0].astype(jnp.int32)
    e2 = r[:, 1].astype(jnp.int32)
    block_e, block_nv, slot_tok, pos = _route_slots(e1, e2, MOE_TB, tm_row)
    y_slots = _moe(x2, block_e, block_nv, slot_tok, moe_w_gate[layer].astype(F32),
                   moe_w_up[layer].astype(F32), moe_w_down[layer].astype(F32))

    ln3 = _pad_rows(jnp.stack([ln3_g[layer], ln3_b[layer]], axis=0).astype(F32))
    y_first, y_rest = _combine(y_slots, pos, x2, r, ln3, tm_row, nb_first * t)
    return y_first.reshape(nb_first, t, D_MODEL), y_rest.reshape(nb - nb_first, t, D_MODEL)


def kernel(x_prompt, x_sample, mem_prompt, mem_sample, w_in, hgrn_lb, hgrn_norm_w, gdn_conv_w, gdn_a_log, gdn_dt_bias, gdn_norm_w, w_out, ln1_g, ln1_b, xa_w_q, xa_w_kv, xa_w_o, ln2_g, ln2_b, moe_w_group, moe_b_group, moe_w_expert, moe_b_expert, moe_w_gate, moe_w_up, moe_w_down, ln3_g, ln3_b):
    assert x_prompt.shape[1] == x_sample.shape[1]
    mem = jnp.concatenate([mem_prompt, mem_sample], axis=0)
    return _layer(x_prompt, x_sample, mem, w_in, hgrn_lb, hgrn_norm_w, gdn_conv_w, gdn_a_log, gdn_dt_bias,
                  gdn_norm_w, w_out, ln1_g, ln1_b, xa_w_q, xa_w_kv, xa_w_o, ln2_g, ln2_b, moe_w_group, moe_b_group,
                  moe_w_expert, moe_b_expert, moe_w_gate, moe_w_up, moe_w_down, ln3_g, ln3_b)
```

```python
import functools
import math

import numpy as np
import jax
import jax.numpy as jnp
from jax import lax
from jax.experimental import pallas as pl
from jax.experimental.pallas import tpu as pltpu

F32 = jnp.float32
BF16 = jnp.bfloat16

D_MODEL = 2048
N_HEADS = 8
D_HEAD = 128
SEG = N_HEADS * D_HEAD
CONV_K = 5
XA_HEADS = 4
XA_HEAD_DIM = D_MODEL // XA_HEADS
N_GROUPS = 4
EXPERTS_PER_GROUP = 8
N_EXPERTS = N_GROUPS * EXPERTS_PER_GROUP
D_EXPERT = 512
DN_ALPHA = 2.0 ** 0.25
LN_EPS = 1e-5
NORM_EPS = 1e-6
LOG2E = math.log2(math.e)

LANES = 128
SUBLANES = 8
VMEM_LIMIT = 56 * 1024 * 1024

HG_CHUNK = 128
HG_TBK = 256
HG_HPG = 8
HG_SMALL_LEVELS = 3
GD_CHUNK = 128
GD_TBK = 256
GD_HPG = 8
MOE_TB = 512


def _cparams(sem):
    return pltpu.CompilerParams(dimension_semantics=sem, vmem_limit_bytes=VMEM_LIMIT)


def _dot(a, b):
    return jnp.dot(a, b, preferred_element_type=F32)


def _dot_nt(a, b):
    return lax.dot_general(a, b, (((1,), (1,)), ((), ())), preferred_element_type=F32)


def _dot_tn(a, b):
    return lax.dot_general(a, b, (((0,), (0,)), ((), ())), preferred_element_type=F32)


def _split2(x):
    hi = x.astype(BF16)
    mid = (x - hi.astype(F32)).astype(BF16)
    return hi, mid


def _sigmoid(x):
    return 1.0 / (1.0 + jnp.exp(-x))


def _layernorm_rows(y, g, b):
    mu = jnp.mean(y, axis=-1, keepdims=True)
    yc = y - mu
    var = jnp.mean(yc * yc, axis=-1, keepdims=True)
    return yc * lax.rsqrt(var + LN_EPS) * g + b


def _mm_kernel(x_ref, w_ref, o_ref):
    o_ref[...] = _dot(x_ref[...], w_ref[...]).astype(o_ref.dtype)


def _matmul(x, w, tm, tn, out_dtype):
    m, k = x.shape
    n = w.shape[1]
    return pl.pallas_call(
        _mm_kernel,
        grid=(m // tm, n // tn),
        in_specs=[pl.BlockSpec((tm, k), lambda i, j: (i, 0)),
                  pl.BlockSpec((k, tn), lambda i, j: (0, j))],
        out_specs=pl.BlockSpec((tm, tn), lambda i, j: (i, j)),
        out_shape=jax.ShapeDtypeStruct((m, n), out_dtype),
        compiler_params=_cparams(("parallel", "arbitrary")),
        name="dense_matmul",
    )(x, w)


def _mm2_kernel(xa_ref, xb_ref, w_ref, ws_ref, o_ref, os_ref, *, na):
    i = pl.program_id(0)
    j = pl.program_id(1)

    def project(x_ref):
        xb16 = x_ref[...].astype(BF16)
        o_ref[...] = _dot(xb16, w_ref[...])

        @pl.when(j == 0)
        def _():
            os_ref[...] = _dot(xb16, ws_ref[...])

    @pl.when(i < na)
    def _():
        project(xa_ref)

    @pl.when(i >= na)
    def _():
        project(xb_ref)


def _in_proj(xa, xb, w, w_small, tm, tn):
    k = xa.shape[1]
    n = w.shape[1]
    ns = w_small.shape[1]
    na = xa.shape[0] // tm
    nt = na + xb.shape[0] // tm
    return pl.pallas_call(
        functools.partial(_mm2_kernel, na=na),
        grid=(nt, n // tn),
        in_specs=[pl.BlockSpec((tm, k), lambda i, j: (jnp.minimum(i, na - 1), 0)),
                  pl.BlockSpec((tm, k), lambda i, j: (jnp.maximum(i - na, 0), 0)),
                  pl.BlockSpec((k, tn), lambda i, j: (0, j)),
                  pl.BlockSpec((k, ns), lambda i, j: (0, 0))],
        out_specs=[pl.BlockSpec((tm, tn), lambda i, j: (i, j)),
                   pl.BlockSpec((tm, ns), lambda i, j: (i, 0))],
        out_shape=[jax.ShapeDtypeStruct((nt * tm, n), F32), jax.ShapeDtypeStruct((nt * tm, ns), F32)],
        compiler_params=_cparams(("parallel", "arbitrary")),
        name="in_proj_matmul",
    )(xa, xb, w, w_small)


def _hgrn_constants(c):
    nl = int(math.log2(c))
    t = np.arange(c)
    u = np.arange(c)
    blocks = []
    blocks.append((u[None, :] <= t[:, None]).astype(np.float32))
    masks, qrows = [], []
    for l in range(nl):
        cc = 1 << l
        m = (t // (2 * cc)) * (2 * cc) + cc
        upper = ((t >> l) & 1) == 1
        a = np.zeros((c, c), np.float32)
        for r in range(c):
            if upper[r]:
                a[r, m[r]:r + 1] = 1.0
            else:
                a[r, r + 1:m[r]] = 1.0
        if 1 <= l < HG_SMALL_LEVELS:
            blocks.append(a)
        same = (t[:, None] >> (l + 1)) == (t[None, :] >> (l + 1))
        masks.append((same & upper[:, None] & (~upper)[None, :]).astype(np.float32))
        qrows.append(np.broadcast_to(upper[:, None], (c, D_HEAD)).astype(np.float32))
    blocks.append(np.ones((16, c), np.float32))
    a_f = np.concatenate(blocks, axis=0)
    m_f = np.stack(masks)
    q_f = np.stack(qrows)
    a_b = np.concatenate([b[::-1, ::-1] for b in blocks], axis=0)
    m_b = m_f[:, ::-1, ::-1]
    q_b = q_f[:, ::-1, :]
    return (np.stack([a_f, a_b]), np.stack([m_f, m_b]).copy(), np.stack([q_f, q_b]).copy(), nl)


def _hgrn_kernel(q_ref, f_ref, v_ref, lb_ref, a_ref, m_ref, qr_ref, o_ref, st_ref, *, c, nck, nl, scale, hpg):
    d = pl.program_id(0)
    i = pl.program_id(3)

    @pl.when(i == 0)
    def _():
        st_ref[...] = jnp.zeros_like(st_ref)

    probs = [(j, g) for j in range(nck) for g in range(hpg)]

    def rows(j):
        jj = jnp.where(d == 0, j, nck - 1 - j)
        return pl.ds(pl.multiple_of(jj * c, c), c)

    def cols(g):
        return slice(g * D_HEAD, (g + 1) * D_HEAD)

    def eblk(k, g2, rows=c):
        e2 = _dot(a_ref[pl.ds(k * c, rows), :], g2)
        return e2[:, :D_HEAD] + e2[:, D_HEAD:]

    def boundary(gcum, l):
        cc = 1 << l
        pieces = []
        for p in range(c // (2 * cc)):
            m = p * 2 * cc + cc
            rowv = jnp.where(d == 0, gcum[m - 1:m, :], gcum[m:m + 1, :])
            pieces.append(jnp.broadcast_to(rowv, (2 * cc, D_HEAD)))
        return pieces[0] if len(pieces) == 1 else jnp.concatenate(pieces, axis=0)

    qs, kk, ff, vb, vv, g2, gcum, gtot = {}, {}, {}, {}, {}, {}, {}, {}
    for (j, g) in probs:
        lb = lb_ref[g, 0:1, :]
        one_m_lb = 1.0 - lb
        fr = f_ref[rows(j), cols(g)]
        qs[j, g] = q_ref[rows(j), cols(g)] * scale
        v = v_ref[rows(j), cols(g)]
        vv[j, g] = v
        vb[j, g] = v.astype(BF16)
        sig = _sigmoid(fr)
        ff[j, g] = lb + one_m_lb * sig
        kk[j, g] = one_m_lb * (1.0 - sig)
        gl = jnp.log(ff[j, g]) * LOG2E
        g_hi, g_mid = _split2(gl)
        g2[j, g] = jnp.concatenate([g_hi, g_mid], axis=1)
    for p in probs:
        gcum[p] = eblk(0, g2[p])
        gtot[p] = eblk(HG_SMALL_LEVELS, g2[p], rows=16)[0:1, :]

    s = {}
    for l in range(nl):
        if l == 0:
            wq = {p: (qs[p] * ff[p]).astype(BF16) for p in probs}
            wk = {p: kk[p].astype(BF16) for p in probs}
        else:
            if l < HG_SMALL_LEVELS:
                z = {p: jnp.exp2(eblk(l, g2[p])) for p in probs}
            else:
                sgn = 2.0 * qr_ref[l] - 1.0
                z = {p: jnp.exp2((gcum[p] - boundary(gcum[p], l)) * sgn) for p in probs}
            wq = {p: (qs[p] * z[p]).astype(BF16) for p in probs}
            wk = {p: (kk[p] * z[p]).astype(BF16) for p in probs}
        sd = {p: _dot_nt(wq[p], wk[p]) for p in probs}
        for p in probs:
            s[p] = m_ref[l] * sd[p] if l == 0 else s[p] + m_ref[l] * sd[p]

    o_in, qd, upd, dec = {}, {}, {}, {}
    for p in probs:
        dg = jnp.sum(qs[p] * kk[p], axis=-1, keepdims=True)
        o_in[p] = _dot(s[p].astype(BF16), vb[p]) + dg * vv[p]
        qd[p] = (qs[p] * jnp.exp2(gcum[p])).astype(BF16)
        kd = (kk[p] * jnp.exp2(gtot[p] - gcum[p])).astype(BF16)
        upd[p] = _dot_tn(vb[p], kd)
        dec[p] = jnp.exp2(gtot[p])

    st = [st_ref[g] for g in range(hpg)]
    for j in range(nck):
        for g in range(hpg):
            p = (j, g)
            o_ref[rows(j), cols(g)] = (o_in[p] + _dot_nt(qd[p], st[g].astype(BF16))).astype(o_ref.dtype)
            st[g] = st[g] * dec[p] + upd[p]
    for g in range(hpg):
        st_ref[g] = st[g]


def _hgrn2(proj, lb2, nb, t, tbk, hpg):
    c = HG_CHUNK
    n = nb * t
    nblk = t // tbk
    a_np, m_np, q_np, nl = _hgrn_constants(c)
    a_all = jnp.asarray(a_np, BF16)
    masks = jnp.asarray(m_np, F32)
    qrows = jnp.asarray(q_np, F32)
    lb8 = jnp.broadcast_to(lb2.reshape(2, N_HEADS, 1, D_HEAD), (2, N_HEADS, SUBLANES, D_HEAD))
    ngrp = N_HEADS // hpg
    wb = hpg * D_HEAD

    def row(d, b, i):
        return b * nblk + jnp.where(d == 0, i, nblk - 1 - i)

    kern = functools.partial(_hgrn_kernel, c=c, nck=tbk // c, nl=nl, scale=D_HEAD ** -0.5, hpg=hpg)
    na = a_np.shape[1]
    return pl.pallas_call(
        kern,
        grid=(2, nb, ngrp, nblk),
        in_specs=[
            pl.BlockSpec((tbk, wb), lambda d, b, h, i: (row(d, b, i), h)),
            pl.BlockSpec((tbk, wb), lambda d, b, h, i: (row(d, b, i), (1 + d) * ngrp + h)),
            pl.BlockSpec((tbk, wb), lambda d, b, h, i: (row(d, b, i), 3 * ngrp + h)),
            pl.BlockSpec((None, hpg, SUBLANES, D_HEAD), lambda d, b, h, i: (d, h, 0, 0)),
            pl.BlockSpec((None, na, c), lambda d, b, h, i: (d, 0, 0)),
            pl.BlockSpec((None, nl, c, c), lambda d, b, h, i: (d, 0, 0, 0)),
            pl.BlockSpec((None, nl, c, D_HEAD), lambda d, b, h, i: (d, 0, 0, 0)),
        ],
        out_specs=pl.BlockSpec((None, tbk, wb), lambda d, b, h, i: (d, row(d, b, i), h)),
        out_shape=jax.ShapeDtypeStruct((2, n, SEG), BF16),
        scratch_shapes=[pltpu.VMEM((hpg, D_HEAD, D_HEAD), F32)],
        compiler_params=_cparams(("parallel", "parallel", "parallel", "arbitrary")),
        name="hgrn2_bidir",
    )(proj, proj, proj, lb8, a_all, masks, qrows)


def _gdn_prep_kernel(prev_ref, cur_ref, nxt_ref, w_ref, o_ref, *, tbk, scale):
    i = pl.program_id(1)
    s = pl.program_id(2)
    nblk = pl.num_programs(1)
    prev = jnp.where(i == 0, 0.0, prev_ref[...])
    nxt = jnp.where(i == nblk - 1, 0.0, nxt_ref[...])
    full = jnp.concatenate([prev, cur_ref[...], nxt], axis=0)
    rows = tbk + 2 * SUBLANES
    acc = jnp.zeros((tbk, SEG), F32)
    for j in range(CONV_K):
        shift = (CONV_K // 2 - j) % rows
        sh = full if shift == 0 else pltpu.roll(full, shift, axis=0)
        acc = acc + sh[SUBLANES:SUBLANES + tbk, :] * w_ref[j:j + 1, :]
    y = acc * _sigmoid(acc)
    fac_all = jnp.where(s == 0, scale, 1.0)
    for h in range(N_HEADS):
        ys = y[:, h * D_HEAD:(h + 1) * D_HEAD]
        ss = jnp.sum(ys * ys, axis=-1, keepdims=True)
        fac = jnp.where(s < 2, lax.rsqrt(ss + NORM_EPS), 1.0) * fac_all
        o_ref[:, h * D_HEAD:(h + 1) * D_HEAD] = ys * fac


def _gdn_prep(proj, conv_w, nb, t, tbk):
    n = nb * t
    nblk = t // tbk
    hb = tbk // SUBLANES
    w8 = jnp.concatenate([conv_w, jnp.zeros((SUBLANES - CONV_K, 3 * SEG), F32)], axis=0)
    last = n // SUBLANES - 1
    kern = functools.partial(_gdn_prep_kernel, tbk=tbk, scale=D_HEAD ** -0.5)
    return pl.pallas_call(
        kern,
        grid=(nb, nblk, 3),
        in_specs=[
            pl.BlockSpec((SUBLANES, SEG), lambda b, i, s: (jnp.maximum((b * nblk + i) * hb - 1, 0), 5 + s)),
            pl.BlockSpec((tbk, SEG), lambda b, i, s: (b * nblk + i, 5 + s)),
            pl.BlockSpec((SUBLANES, SEG), lambda b, i, s: (jnp.minimum((b * nblk + i + 1) * hb, last), 5 + s)),
            pl.BlockSpec((SUBLANES, SEG), lambda b, i, s: (0, s)),
        ],
        out_specs=pl.BlockSpec((None, tbk, SEG), lambda b, i, s: (s, b * nblk + i, 0)),
        out_shape=jax.ShapeDtypeStruct((3, n, SEG), F32),
        compiler_params=_cparams(("parallel", "parallel", "arbitrary")),
        name="gdn_conv_prep",
    )(proj, proj, proj, w8)


def _gdn_constants(c):
    nl = int(math.log2(c))
    t = np.arange(c)
    incl = (t[None, :] <= t[:, None]).astype(np.float32)
    strict = (t[None, :] < t[:, None]).astype(np.float32)
    after = (t[:, None] > t[None, :]).astype(np.float32)
    rhs_f = np.concatenate([np.ones((c, LANES), np.float32), after], axis=1)
    lvl = []
    for l in range(nl):
        upper = ((t >> l) & 1) == 1
        same = (t[:, None] >> (l + 1)) == (t[None, :] >> (l + 1))
        lvl.append((same & upper[:, None] & (~upper)[None, :]).astype(np.float32))
    lvl = np.stack(lvl)
    mk_f = np.concatenate([incl[None], strict[None], lvl], axis=0)
    mk_b = mk_f[:, ::-1, ::-1]
    rhs_b = np.concatenate([np.ones((c, LANES), np.float32), after[::-1, ::-1]], axis=1)
    return np.stack([mk_f, mk_b]).copy(), np.stack([rhs_f, rhs_b]).copy(), nl


def _gdn_kernel(coef_ref, q_ref, k_ref, v_ref, arow_ref, ps_ref, mk_ref, rhs_ref, o_ref, st_ref, *, c, nck, nl, hpg):
    d = pl.program_id(0)
    hg = pl.program_id(2)
    i = pl.program_id(3)

    @pl.when(i == 0)
    def _():
        st_ref[...] = jnp.zeros_like(st_ref)

    lane = lax.broadcasted_iota(jnp.int32, (c, LANES), 1)
    incl = mk_ref[0]
    strict = mk_ref[1]
    eye = incl - strict
    incl_b = incl.astype(BF16)
    rhs = rhs_ref[...]
    probs = [(j, g) for j in range(nck) for g in range(hpg)]

    def rows(j):
        jj = jnp.where(d == 0, j, nck - 1 - j)
        return pl.ds(pl.multiple_of(jj * c, c), c)

    def cols(g):
        return slice(g * D_HEAD, (g + 1) * D_HEAD)

    qn, kn, vv, beta, gcum, dec, etot = {}, {}, {}, {}, {}, {}, {}
    for (j, g) in probs:
        h = hg * hpg + g
        neg_ea = coef_ref[0, d * N_HEADS + h]
        dtb = coef_ref[1, d * N_HEADS + h]
        a_blk = arow_ref[pl.ds(h, 1), :]
        a_row = jnp.where(d == 0, a_blk[:, j * c:(j + 1) * c], a_blk[:, (nck - 1 - j) * c:(nck - j) * c])
        z = a_row + dtb
        g_row = neg_ea * (jnp.maximum(z, 0.0) + jnp.log(1.0 + jnp.exp(-jnp.abs(z))))
        beta_lane = 2 * N_HEADS + d * N_HEADS + h
        b_col = jnp.sum(jnp.where(lane == beta_lane, ps_ref[rows(j), :], 0.0), axis=-1, keepdims=True)
        beta[j, g] = _sigmoid(b_col)
        g_hi, g_mid = _split2(g_row)
        out = _dot(jnp.concatenate([incl_b * g_hi, incl_b * g_mid], axis=0), rhs)
        out = out[:c] + out[c:]
        gcum[j, g] = out[:, :LANES]
        dmat = out[:, LANES:]
        dec[j, g] = jnp.exp2(jnp.minimum(dmat, 0.0)) * incl
        qn[j, g] = q_ref[rows(j), cols(g)]
        kn[j, g] = k_ref[rows(j), cols(g)]
        vv[j, g] = v_ref[rows(j), cols(g)]

    kb, knb, a_dec, tinv = {}, {}, {}, {}
    for p in probs:
        kb[p] = kn[p] * beta[p]
        knb[p] = kn[p].astype(BF16)
        a_dec[p] = _dot_nt(kb[p].astype(BF16), knb[p]) * dec[p]
        tinv[p] = eye - a_dec[p] * mk_ref[2]
    a_b = {p: a_dec[p].astype(BF16) for p in probs}
    for l in range(1, nl):
        ml_b = mk_ref[2 + l].astype(BF16)
        tb = {p: tinv[p].astype(BF16) for p in probs}
        ta = {p: _dot(tb[p], a_b[p] * ml_b).astype(BF16) for p in probs}
        for p in probs:
            tinv[p] = tinv[p] - _dot(ta[p], tb[p])

    u, w, qk, qe, ke = {}, {}, {}, {}, {}
    for p in probs:
        gtot = jnp.where(d == 0, gcum[p][c - 1:c, :], gcum[p][0:1, :])
        eg = jnp.exp2(gcum[p])
        etot[p] = jnp.exp2(gtot)
        uw = _dot(tinv[p].astype(BF16), jnp.concatenate([vv[p] * beta[p], kb[p] * eg], axis=1).astype(BF16))
        u[p] = uw[:, :D_HEAD]
        w[p] = uw[:, D_HEAD:].astype(BF16)
        qk[p] = (_dot_nt(qn[p].astype(BF16), knb[p]) * dec[p]).astype(BF16)
        qe[p] = (qn[p] * eg).astype(BF16)
        ke[p] = (kn[p] * jnp.exp2(gtot - gcum[p])).astype(BF16)

    st = [st_ref[g] for g in range(hpg)]
    for j in range(nck):
        stb = [st[g].astype(BF16) for g in range(hpg)]
        ws = [_dot(w[j, g], stb[g]) for g in range(hpg)]
        qs = [_dot(qe[j, g], stb[g]) for g in range(hpg)]
        vnb = [(u[j, g] - ws[g]).astype(BF16) for g in range(hpg)]
        for g in range(hpg):
            o_ref[rows(j), cols(g)] = (qs[g] + _dot(qk[j, g], vnb[g])).astype(o_ref.dtype)
        for g in range(hpg):
            st[g] = st[g] * etot[j, g] + _dot_tn(ke[j, g], vnb[g])
    for g in range(hpg):
        st_ref[g] = st[g]


def _gdn(qkv, ps, ps_t, coef, nb, t, tbk, hpg):
    c = GD_CHUNK
    n = nb * t
    nblk = t // tbk
    mk_np, rhs_np, nl = _gdn_constants(c)
    mk = jnp.asarray(mk_np, F32)
    rhs = jnp.asarray(rhs_np, BF16)
    wb = hpg * D_HEAD

    def row(d, b, i):
        return b * nblk + jnp.where(d == 0, i, nblk - 1 - i)

    kern = functools.partial(_gdn_kernel, c=c, nck=tbk // c, nl=nl, hpg=hpg)
    grid_spec = pltpu.PrefetchScalarGridSpec(
        num_scalar_prefetch=0,
        grid=(2, nb, N_HEADS // hpg, nblk),
        in_specs=[
            pl.BlockSpec(memory_space=pltpu.SMEM),
            pl.BlockSpec((None, tbk, wb), lambda d, b, h, i: (0, row(d, b, i), h)),
            pl.BlockSpec((None, tbk, wb), lambda d, b, h, i: (1, row(d, b, i), h)),
            pl.BlockSpec((None, tbk, wb), lambda d, b, h, i: (2, row(d, b, i), h)),
            pl.BlockSpec((SUBLANES, tbk), lambda d, b, h, i: (d, row(d, b, i))),
            pl.BlockSpec((tbk, LANES), lambda d, b, h, i: (row(d, b, i), 0)),
            pl.BlockSpec((None, 2 + nl, c, c), lambda d, b, h, i: (d, 0, 0, 0)),
            pl.BlockSpec((None, c, c + LANES), lambda d, b, h, i: (d, 0, 0)),
        ],
        out_specs=pl.BlockSpec((None, tbk, wb), lambda d, b, h, i: (d, row(d, b, i), h)),
        scratch_shapes=[pltpu.VMEM((hpg, D_HEAD, D_HEAD), F32)],
    )
    return pl.pallas_call(
        kern,
        grid_spec=grid_spec,
        out_shape=jax.ShapeDtypeStruct((2, n, SEG), BF16),
        compiler_params=_cparams(("parallel", "parallel", "parallel", "arbitrary")),
        name="gdn_bidir",
    )(coef, qkv, qkv, qkv, ps_t, ps, mk, rhs)


def _mixout_kernel(oh_ref, og_ref, hg_ref, gz_ref, xa_ref, xb_ref, w_ref, nw_ref, ln_ref, o_ref, lhs_ref, *, na):
    i = pl.program_id(0)

    def gated(o2_ref, gate_ref, wrow, col0):
        for h in range(N_HEADS):
            sl = slice(h * D_HEAD, (h + 1) * D_HEAD)
            o = o2_ref[0, :, sl].astype(F32) + o2_ref[1, :, sl].astype(F32)
            ms = jnp.mean(o * o, axis=-1, keepdims=True)
            gt = gate_ref[:, sl]
            y = o * lax.rsqrt(ms + NORM_EPS) * wrow * (gt * _sigmoid(gt))
            lhs_ref[:, col0 + h * D_HEAD:col0 + (h + 1) * D_HEAD] = y.astype(BF16)

    gated(oh_ref, hg_ref, nw_ref[0:1, :], 0)
    gated(og_ref, gz_ref, nw_ref[1:2, :], SEG)
    hmix = _dot(lhs_ref[...], w_ref[...])
    x = jnp.where(i < na, xa_ref[...], xb_ref[...])
    y = DN_ALPHA * x + hmix
    o_ref[...] = _layernorm_rows(y, ln_ref[0:1, :], ln_ref[1:2, :])


def _mixout(oh, og, proj, xa, xb, w_out_b, nw, ln, tm):
    na = xa.shape[0] // tm
    n = xa.shape[0] + xb.shape[0]
    return pl.pallas_call(
        functools.partial(_mixout_kernel, na=na),
        grid=(n // tm,),
        in_specs=[
            pl.BlockSpec((2, tm, SEG), lambda i: (0, i, 0)),
            pl.BlockSpec((2, tm, SEG), lambda i: (0, i, 0)),
            pl.BlockSpec((tm, SEG), lambda i: (i, 4)),
            pl.BlockSpec((tm, SEG), lambda i: (i, 8)),
            pl.BlockSpec((tm, D_MODEL), lambda i: (jnp.minimum(i, na - 1), 0)),
            pl.BlockSpec((tm, D_MODEL), lambda i: (jnp.maximum(i - na, 0), 0)),
            pl.BlockSpec((2 * SEG, D_MODEL), lambda i: (0, 0)),
            pl.BlockSpec((SUBLANES, D_HEAD), lambda i: (0, 0)),
            pl.BlockSpec((SUBLANES, D_MODEL), lambda i: (0, 0)),
        ],
        out_specs=pl.BlockSpec((tm, D_MODEL), lambda i: (i, 0)),
        out_shape=jax.ShapeDtypeStruct((n, D_MODEL), F32),
        scratch_shapes=[pltpu.VMEM((tm, 2 * SEG), BF16)],
        compiler_params=_cparams(("parallel",)),
        name="mixer_out_ln1",
    )(oh, og, proj, proj, xa, xb, w_out_b, nw, ln)


def _attn_kernel(x_ref, kv_ref, wq_ref, wo_ref, ln_ref, wr_ref, br_ref, o_ref, r_ref, att_ref):
    x1 = x_ref[...]
    q = _dot(x1.astype(BF16), wq_ref[...])
    sc = XA_HEAD_DIM ** -0.5
    for h in range(XA_HEADS):
        sl = slice(h * XA_HEAD_DIM, (h + 1) * XA_HEAD_DIM)
        kh = kv_ref[:, sl]
        vh = kv_ref[:, D_MODEL + h * XA_HEAD_DIM:D_MODEL + (h + 1) * XA_HEAD_DIM]
        s = _dot_nt(q[:, sl].astype(BF16), kh) * sc
        s = s - jnp.max(s, axis=-1, keepdims=True)
        p = jnp.exp(s)
        p = p / jnp.sum(p, axis=-1, keepdims=True)
        att_ref[:, sl] = _dot(p.astype(BF16), vh).astype(BF16)
    hx = _dot(att_ref[...], wo_ref[...])
    x2 = _layernorm_rows(DN_ALPHA * x1 + hx, ln_ref[0:1, :], ln_ref[1:2, :])
    o_ref[...] = x2

    x_hi, x_mid = _split2(x2)
    lg2 = _dot(x_hi, wr_ref[...])
    lg = lg2[:, :LANES] + lg2[:, LANES:] + _dot(x_mid, wr_ref[:, :LANES]) + br_ref[0:1, :]
    tm = lg.shape[0]
    lane_i = lax.broadcasted_iota(jnp.int32, (tm, LANES), 1)
    lane = lane_i.astype(F32)
    grp_of_lane = (lane_i >> 3).astype(F32)
    neg = jnp.float32(-1e30)
    big = jnp.float32(LANES)
    is_g = (lane_i >= N_EXPERTS) & (lane_i < N_EXPERTS + N_GROUPS)
    gl = jnp.where(is_g, lg, neg)
    gmax = jnp.max(gl, axis=-1, keepdims=True)
    gsel = jnp.min(jnp.where(gl == gmax, lane, big), axis=-1, keepdims=True) - N_EXPERTS
    p_group = 1.0 / jnp.sum(jnp.where(is_g, jnp.exp(gl - gmax), 0.0), axis=-1, keepdims=True)
    in_grp = (lane_i < N_EXPERTS) & (grp_of_lane == gsel)
    el = jnp.where(in_grp, lg, neg)
    m1 = jnp.max(el, axis=-1, keepdims=True)
    i1 = jnp.min(jnp.where(el == m1, lane, big), axis=-1, keepdims=True)
    el2 = jnp.where(lane == i1, neg, el)
    m2 = jnp.max(el2, axis=-1, keepdims=True)
    i2 = jnp.min(jnp.where(el2 == m2, lane, big), axis=-1, keepdims=True)
    e2 = jnp.exp(m2 - m1)
    g1 = p_group / (1.0 + e2)
    g2 = p_group * e2 / (1.0 + e2)
    r = jnp.where(lane_i == 0, i1, 0.0)
    r = jnp.where(lane_i == 1, i2, r)
    r = jnp.where(lane_i == 2, g1, r)
    r = jnp.where(lane_i == 3, g2, r)
    r_ref[...] = r


def _attn(x1, kv, wq_b, wo_b, ln, wr, br, nb, t, tm):
    n = nb * t
    nblk = t // tm
    n_mem = kv.shape[0] // nb
    return pl.pallas_call(
        _attn_kernel,
        grid=(nb, nblk),
        in_specs=[
            pl.BlockSpec((tm, D_MODEL), lambda b, i: (b * nblk + i, 0)),
            pl.BlockSpec((n_mem, 2 * D_MODEL), lambda b, i: (b, 0)),
            pl.BlockSpec((D_MODEL, D_MODEL), lambda b, i: (0, 0)),
            pl.BlockSpec((D_MODEL, D_MODEL), lambda b, i: (0, 0)),
            pl.BlockSpec((SUBLANES, D_MODEL), lambda b, i: (0, 0)),
            pl.BlockSpec((D_MODEL, 2 * LANES), lambda b, i: (0, 0)),
            pl.BlockSpec((SUBLANES, LANES), lambda b, i: (0, 0)),
        ],
        out_specs=[pl.BlockSpec((tm, D_MODEL), lambda b, i: (b * nblk + i, 0)),
                   pl.BlockSpec((tm, LANES), lambda b, i: (b * nblk + i, 0))],
        out_shape=[jax.ShapeDtypeStruct((n, D_MODEL), F32), jax.ShapeDtypeStruct((n, LANES), F32)],
        scratch_shapes=[pltpu.VMEM((tm, D_MODEL), BF16)],
        compiler_params=_cparams(("parallel", "arbitrary")),
        name="xattn_ln2_router",
    )(x1, kv, wq_b, wo_b, ln, wr, br)


DMA_UNROLL = 8


def _issue_row_gather(idx_ref, src_hbm, dst, sem, n_rows):
    def body(k, carry):
        for u in range(DMA_UNROLL):
            r = k * DMA_UNROLL + u
            pltpu.make_async_copy(src_hbm.at[pl.ds(idx_ref[0, r], 1)], dst.at[pl.ds(r, 1)], sem).start(priority=u % 2)
        return carry

    lax.fori_loop(0, n_rows // DMA_UNROLL, body, 0)


def _issue_row_gather_inline(idx_ref, src_hbm, dst, sem, n_rows):
    for r in range(n_rows):
        pltpu.make_async_copy(src_hbm.at[pl.ds(idx_ref[0, r], 1)], dst.at[pl.ds(r, 1)], sem).start(priority=r % 2)


def _wait_row_gather(src_hbm, dst, sem, n_rows):
    pltpu.make_async_copy(src_hbm.at[pl.ds(0, n_rows)], dst, sem).wait()


def _moe_kernel(be_ref, nv_ref, tok_ref, tokn_ref, x_hbm, wg_ref, wu_ref, wd_ref, o_ref, xbuf, sem,
                wgb, wub, wdb, *, tb):
    b = pl.program_id(0)
    nb = pl.num_programs(0)
    slot = b % 2

    @pl.when((b == 0) | (be_ref[b] != be_ref[jnp.maximum(b - 1, 0)]))
    def _():
        wgb[...] = wg_ref[...].astype(BF16)
        wub[...] = wu_ref[...].astype(BF16)
        wdb[...] = wd_ref[...].astype(BF16)

    @pl.when((b == 0) & (nv_ref[0] > 0))
    def _():
        _issue_row_gather(tok_ref, x_hbm, xbuf.at[0], sem.at[0], tb)

    has_next = (b + 1 < nb) & (nv_ref[jnp.minimum(b + 1, nb - 1)] > 0)

    def compute():
        xb = xbuf[slot].astype(BF16)
        h1 = _dot(xb, wgb[...])
        h2 = _dot(xb, wub[...])
        hdn = (h1 * _sigmoid(h1) * h2).astype(BF16)
        o_ref[...] = _dot(hdn, wdb[...])

    @pl.when((nv_ref[b] > 0) & has_next)
    def _():
        _wait_row_gather(x_hbm, xbuf.at[slot], sem.at[slot], tb)
        compute()
        _issue_row_gather_inline(tokn_ref, x_hbm, xbuf.at[1 - slot], sem.at[1 - slot], tb)

    @pl.when((nv_ref[b] > 0) & jnp.logical_not(has_next))
    def _():
        _wait_row_gather(x_hbm, xbuf.at[slot], sem.at[slot], tb)
        compute()

    @pl.when((nv_ref[b] == 0) & has_next)
    def _():
        _issue_row_gather(tokn_ref, x_hbm, xbuf.at[1 - slot], sem.at[1 - slot], tb)

    @pl.when(nv_ref[b] == 0)
    def _():
        o_ref[...] = jnp.zeros_like(o_ref)


def _moe(x2, block_e, block_nv, slot_tok, wg_b, wu_b, wd_b):
    tb = MOE_TB
    cap = slot_tok.shape[0]
    n_blocks = cap // tb
    tok3 = slot_tok.reshape(n_blocks, 1, tb)
    grid_spec = pltpu.PrefetchScalarGridSpec(
        num_scalar_prefetch=2,
        grid=(n_blocks,),
        in_specs=[
            pl.BlockSpec((None, 1, tb), lambda b, be, nv: (b, 0, 0), memory_space=pltpu.SMEM),
            pl.BlockSpec((None, 1, tb), lambda b, be, nv: (jnp.minimum(b + 1, n_blocks - 1), 0, 0),
                         memory_space=pltpu.SMEM),
            pl.BlockSpec(memory_space=pl.ANY),
            pl.BlockSpec((None, D_MODEL, D_EXPERT), lambda b, be, nv: (be[b], 0, 0)),
            pl.BlockSpec((None, D_MODEL, D_EXPERT), lambda b, be, nv: (be[b], 0, 0)),
            pl.BlockSpec((None, D_EXPERT, D_MODEL), lambda b, be, nv: (be[b], 0, 0)),
        ],
        out_specs=pl.BlockSpec((tb, D_MODEL), lambda b, be, nv: (b, 0)),
        scratch_shapes=[pltpu.VMEM((2, tb, D_MODEL), F32), pltpu.SemaphoreType.DMA((2,)),
                        pltpu.VMEM((D_MODEL, D_EXPERT), BF16), pltpu.VMEM((D_MODEL, D_EXPERT), BF16),
                        pltpu.VMEM((D_EXPERT, D_MODEL), BF16)],
    )
    return pl.pallas_call(
        functools.partial(_moe_kernel, tb=tb),
        grid_spec=grid_spec,
        out_shape=jax.ShapeDtypeStruct((cap, D_MODEL), F32),
        compiler_params=_cparams(("arbitrary",)),
        name="moe_grouped_mlp",
    )(block_e, block_nv, tok3, tok3, x2, wg_b, wu_b, wd_b)


def _combine_kernel(pos_ref, posn_ref, y_hbm, x_ref, r_ref, ln_ref, op_ref, os_ref, ybuf, sem, *, tm, npt):
    i = pl.program_id(0)
    nt = pl.num_programs(0)
    slot = i % 2

    @pl.when(i == 0)
    def _():
        _issue_row_gather(pos_ref, y_hbm, ybuf.at[0], sem.at[0], 2 * tm)

    def finish(issue_next):
        r = r_ref[...]
        y = r[:, 2:3] * ybuf[slot, 0:tm, :] + r[:, 3:4] * ybuf[slot, tm:2 * tm, :]
        res = _layernorm_rows(DN_ALPHA * x_ref[...] + y, ln_ref[0:1, :], ln_ref[1:2, :])
        if issue_next:
            _issue_row_gather_inline(posn_ref, y_hbm, ybuf.at[1 - slot], sem.at[1 - slot], 2 * tm)

        @pl.when(i < npt)
        def _():
            op_ref[...] = res

        @pl.when(i >= npt)
        def _():
            os_ref[...] = res

    @pl.when(i + 1 < nt)
    def _():
        _wait_row_gather(y_hbm, ybuf.at[slot], sem.at[slot], 2 * tm)
        finish(True)

    @pl.when(i + 1 >= nt)
    def _():
        _wait_row_gather(y_hbm, ybuf.at[slot], sem.at[slot], 2 * tm)
        finish(False)


def _combine(y_slots, pos, x2, r, ln, tm, n_first):
    n = x2.shape[0]
    nt = n // tm
    npt = n_first // tm
    grid_spec = pltpu.PrefetchScalarGridSpec(
        num_scalar_prefetch=0,
        grid=(nt,),
        in_specs=[
            pl.BlockSpec((None, 1, 2 * tm), lambda i: (i, 0, 0), memory_space=pltpu.SMEM),
            pl.BlockSpec((None, 1, 2 * tm), lambda i: (jnp.minimum(i + 1, nt - 1), 0, 0), memory_space=pltpu.SMEM),
            pl.BlockSpec(memory_space=pl.ANY),
            pl.BlockSpec((tm, D_MODEL), lambda i: (i, 0)),
            pl.BlockSpec((tm, LANES), lambda i: (i, 0)),
            pl.BlockSpec((SUBLANES, D_MODEL), lambda i: (0, 0)),
        ],
        out_specs=[pl.BlockSpec((tm, D_MODEL), lambda i: (jnp.minimum(i, npt - 1), 0)),
                   pl.BlockSpec((tm, D_MODEL), lambda i: (jnp.maximum(i - npt, 0), 0))],
        scratch_shapes=[pltpu.VMEM((2, 2 * tm, D_MODEL), F32), pltpu.SemaphoreType.DMA((2,))],
    )
    return pl.pallas_call(
        functools.partial(_combine_kernel, tm=tm, npt=npt),
        grid_spec=grid_spec,
        out_shape=[jax.ShapeDtypeStruct((n_first, D_MODEL), F32), jax.ShapeDtypeStruct((n - n_first, D_MODEL), F32)],
        compiler_params=_cparams(("arbitrary",)),
        name="moe_combine_ln3",
    )(pos, pos, y_slots, x2, r, ln)


def _route_slots(e1, e2, tb, tm):
    n = e1.shape[0]
    flat_e = jnp.stack([e1, e2], axis=1).reshape(-1)
    n_assign = 2 * n
    n_blocks = -(-n_assign // tb) + N_EXPERTS
    cap = n_blocks * tb
    experts = jnp.arange(N_EXPERTS, dtype=jnp.int32)
    counts = jnp.sum((flat_e[:, None] == experts[None, :]).astype(jnp.int32), axis=0)
    padded = (counts + tb - 1) // tb * tb
    pad_end = jnp.cumsum(padded)
    need_end = jnp.cumsum(padded - counts)
    fill = jnp.arange(cap - n_assign, dtype=jnp.int32)
    fill_e = jnp.sum((need_end[None, :] <= fill[:, None]).astype(jnp.int32), axis=1)
    keys = jnp.concatenate([flat_e, fill_e])
    a_id = jnp.concatenate([jnp.arange(n_assign, dtype=jnp.int32),
                            n_assign + jnp.arange(cap - n_assign, dtype=jnp.int32)])
    _, slot_a = lax.sort((keys, a_id), num_keys=1, is_stable=True)
    slot_tok = jnp.where(slot_a < n_assign, slot_a // 2, 0).astype(jnp.int32)
    _, slot_of = lax.sort((slot_a, jnp.arange(cap, dtype=jnp.int32)), num_keys=1)
    slot_of = slot_of[:n_assign]
    starts = jnp.arange(n_blocks, dtype=jnp.int32) * tb
    block_e = jnp.minimum(jnp.sum((pad_end[None, :] <= starts[:, None]).astype(jnp.int32), axis=1),
                          N_EXPERTS - 1).astype(jnp.int32)
    block_nv = jnp.sum((slot_a.reshape(n_blocks, tb) < n_assign).astype(jnp.int32), axis=1)
    pos = slot_of.reshape(n // tm, tm, 2).transpose(0, 2, 1).reshape(n // tm, 1, 2 * tm)
    return block_e, block_nv, slot_tok, pos


def _pad_rows(a, rows=SUBLANES):
    return jnp.concatenate([a, jnp.zeros((rows - a.shape[0],) + a.shape[1:], a.dtype)], axis=0)


def _layer(x_first, x_rest, mem, w_in, hgrn_lb, hgrn_norm_w, gdn_conv_w, gdn_a_log, gdn_dt_bias, gdn_norm_w, w_out,
           ln1_g, ln1_b, xa_w_q, xa_w_kv, xa_w_o, ln2_g, ln2_b, moe_w_group, moe_b_group,
           moe_w_expert, moe_b_expert, moe_w_gate, moe_w_up, moe_w_down, ln3_g, ln3_b,
           *, tm_mm=1024, tbk=512, tm_row=256):
    nb_first, t, _ = x_first.shape
    nb = nb_first + x_rest.shape[0]
    xa = x_first.reshape(nb_first * t, D_MODEL)
    xb = x_rest.reshape((nb - nb_first) * t, D_MODEL)
    layer = 0

    w_in_b = w_in[layer].astype(BF16)
    n_main = 9 * SEG
    w_small = jnp.concatenate([w_in_b[:, n_main:], jnp.zeros((D_MODEL, LANES - 4 * N_HEADS), BF16)], axis=1)
    proj, ps = _in_proj(xa, xb, w_in_b[:, :n_main], w_small, tm_mm, SEG)

    lb2 = jnp.cumsum(jax.nn.softmax(hgrn_lb.astype(F32), axis=0), axis=0)[layer]
    oh = _hgrn2(proj, lb2, nb, t, HG_TBK, HG_HPG)

    qkv = _gdn_prep(proj, gdn_conv_w[layer, :, 0, :].astype(F32), nb, t, tbk)
    coef = jnp.stack([-jnp.exp(gdn_a_log[layer].astype(F32)).reshape(-1) * LOG2E,
                      gdn_dt_bias[layer].astype(F32).reshape(-1)], axis=0)
    og = _gdn(qkv, ps, ps.T, coef, nb, t, GD_TBK, GD_HPG)

    nw = _pad_rows(jnp.stack([hgrn_norm_w[layer], gdn_norm_w[layer]], axis=0).astype(F32))
    ln1 = _pad_rows(jnp.stack([ln1_g[layer], ln1_b[layer]], axis=0).astype(F32))
    x1 = _mixout(oh, og, proj, xa, xb, w_out[layer].astype(BF16), nw, ln1, tm_row)

    n_mem = mem.shape[1]
    kv = _matmul(mem.reshape(nb * n_mem, D_MODEL).astype(BF16), xa_w_kv[layer].astype(BF16),
                 n_mem, 1024, BF16)
    ln2 = _pad_rows(jnp.stack([ln2_g[layer], ln2_b[layer]], axis=0).astype(F32))
    w_r = jnp.concatenate([moe_w_expert[layer], moe_w_group[layer],
                           jnp.zeros((D_MODEL, LANES - N_EXPERTS - N_GROUPS), F32)], axis=1).astype(F32)
    w_r_hi = w_r.astype(BF16)
    w_r_mid = (w_r - w_r_hi.astype(F32)).astype(BF16)
    wr = jnp.concatenate([w_r_hi, w_r_mid], axis=1)
    b_r = jnp.concatenate([moe_b_expert[layer], moe_b_group[layer],
                           jnp.zeros((LANES - N_EXPERTS - N_GROUPS,), F32)]).astype(F32)
    br = jnp.broadcast_to(b_r[None, :], (SUBLANES, LANES))
    x2, r = _attn(x1, kv, xa_w_q[layer].astype(BF16), xa_w_o[layer].astype(BF16), ln2, wr, br, nb, t, tm_row)

    e1 = r[:, 0].astype(jnp.int32)
    e2 = r[:, 1].astype(jnp.int32)
    block_e, block_nv, slot_tok, pos = _route_slots(e1, e2, MOE_TB, tm_row)
    y_slots = _moe(x2, block_e, block_nv, slot_tok, moe_w_gate[layer].astype(F32),
                   moe_w_up[layer].astype(F32), moe_w_down[layer].astype(F32))

    ln3 = _pad_rows(jnp.stack([ln3_g[layer], ln3_b[layer]], axis=0).astype(F32))
    y_first, y_rest = _combine(y_slots, pos, x2, r, ln3, tm_row, nb_first * t)
    return y_first.reshape(nb_first, t, D_MODEL), y_rest.reshape(nb - nb_first, t, D_MODEL)


def kernel(x_prompt, x_sample, mem_prompt, mem_sample, w_in, hgrn_lb, hgrn_norm_w, gdn_conv_w, gdn_a_log, gdn_dt_bias, gdn_norm_w, w_out, ln1_g, ln1_b, xa_w_q, xa_w_kv, xa_w_o, ln2_g, ln2_b, moe_w_group, moe_b_group, moe_w_expert, moe_b_expert, moe_w_gate, moe_w_up, moe_w_down, ln3_g, ln3_b):
    assert x_prompt.shape[1] == x_sample.shape[1]
    mem = jnp.concatenate([mem_prompt, mem_sample], axis=0)
    return _layer(x_prompt, x_sample, mem, w_in, hgrn_lb, hgrn_norm_w, gdn_conv_w, gdn_a_log, gdn_dt_bias,
                  gdn_norm_w, w_out, ln1_g, ln1_b, xa_w_q, xa_w_kv, xa_w_o, ln2_g, ln2_b, moe_w_group, moe_b_group,
                  moe_w_expert, moe_b_expert, moe_w_gate, moe_w_up, moe_w_down, ln3_g, ln3_b)
```

```python
import functools
import math

import numpy as np
import jax
import jax.numpy as jnp
from jax import lax
from jax.experimental import pallas as pl
from jax.experimental.pallas import tpu as pltpu

F32 = jnp.float32
BF16 = jnp.bfloat16

D_MODEL = 2048
N_HEADS = 8
D_HEAD = 128
SEG = N_HEADS * D_HEAD
CONV_K = 5
XA_HEADS = 4
XA_HEAD_DIM = D_MODEL // XA_HEADS
N_GROUPS = 4
EXPERTS_PER_GROUP = 8
N_EXPERTS = N_GROUPS * EXPERTS_PER_GROUP
D_EXPERT = 512
DN_ALPHA = 2.0 ** 0.25
LN_EPS = 1e-5
NORM_EPS = 1e-6
LOG2E = math.log2(math.e)

LANES = 128
SUBLANES = 8
VMEM_LIMIT = 56 * 1024 * 1024

HG_CHUNK = 128
HG_TBK = 256
HG_HPG = 8
HG_SMALL_LEVELS = 3
GD_CHUNK = 128
GD_TBK = 256
GD_HPG = 8
MOE_TB = 512
ATT_TM = 512


def _cparams(sem):
    return pltpu.CompilerParams(dimension_semantics=sem, vmem_limit_bytes=VMEM_LIMIT)


def _dot(a, b):
    return jnp.dot(a, b, preferred_element_type=F32)


def _dot_nt(a, b):
    return lax.dot_general(a, b, (((1,), (1,)), ((), ())), preferred_element_type=F32)


def _dot_tn(a, b):
    return lax.dot_general(a, b, (((0,), (0,)), ((), ())), preferred_element_type=F32)


def _split2(x):
    hi = x.astype(BF16)
    mid = (x - hi.astype(F32)).astype(BF16)
    return hi, mid


def _sigmoid(x):
    return 1.0 / (1.0 + jnp.exp(-x))


def _layernorm_rows(y, g, b):
    mu = jnp.mean(y, axis=-1, keepdims=True)
    yc = y - mu
    var = jnp.mean(yc * yc, axis=-1, keepdims=True)
    return yc * lax.rsqrt(var + LN_EPS) * g + b


def _mm_kernel(x_ref, w_ref, o_ref):
    o_ref[...] = _dot(x_ref[...], w_ref[...]).astype(o_ref.dtype)


def _matmul(x, w, tm, tn, out_dtype):
    m, k = x.shape
    n = w.shape[1]
    return pl.pallas_call(
        _mm_kernel,
        grid=(m // tm, n // tn),
        in_specs=[pl.BlockSpec((tm, k), lambda i, j: (i, 0)),
                  pl.BlockSpec((k, tn), lambda i, j: (0, j))],
        out_specs=pl.BlockSpec((tm, tn), lambda i, j: (i, j)),
        out_shape=jax.ShapeDtypeStruct((m, n), out_dtype),
        compiler_params=_cparams(("parallel", "arbitrary")),
        name="dense_matmul",
    )(x, w)


def _mm2_kernel(xa_ref, xb_ref, w_ref, ws_ref, o_ref, os_ref, *, na):
    i = pl.program_id(0)
    j = pl.program_id(1)

    def project(x_ref):
        xb16 = x_ref[...].astype(BF16)
        o_ref[...] = _dot(xb16, w_ref[...])

        @pl.when(j == 0)
        def _():
            os_ref[...] = _dot(xb16, ws_ref[...])

    @pl.when(i < na)
    def _():
        project(xa_ref)

    @pl.when(i >= na)
    def _():
        project(xb_ref)


def _in_proj(xa, xb, w, w_small, tm, tn):
    k = xa.shape[1]
    n = w.shape[1]
    ns = w_small.shape[1]
    na = xa.shape[0] // tm
    nt = na + xb.shape[0] // tm
    return pl.pallas_call(
        functools.partial(_mm2_kernel, na=na),
        grid=(nt, n // tn),
        in_specs=[pl.BlockSpec((tm, k), lambda i, j: (jnp.minimum(i, na - 1), 0)),
                  pl.BlockSpec((tm, k), lambda i, j: (jnp.maximum(i - na, 0), 0)),
                  pl.BlockSpec((k, tn), lambda i, j: (0, j)),
                  pl.BlockSpec((k, ns), lambda i, j: (0, 0))],
        out_specs=[pl.BlockSpec((tm, tn), lambda i, j: (i, j)),
                   pl.BlockSpec((tm, ns), lambda i, j: (i, 0))],
        out_shape=[jax.ShapeDtypeStruct((nt * tm, n), F32), jax.ShapeDtypeStruct((nt * tm, ns), F32)],
        compiler_params=_cparams(("parallel", "arbitrary")),
        name="in_proj_matmul",
    )(xa, xb, w, w_small)


def _hgrn_constants(c):
    nl = int(math.log2(c))
    t = np.arange(c)
    u = np.arange(c)
    blocks = []
    blocks.append((u[None, :] <= t[:, None]).astype(np.float32))
    masks, qrows = [], []
    for l in range(nl):
        cc = 1 << l
        m = (t // (2 * cc)) * (2 * cc) + cc
        upper = ((t >> l) & 1) == 1
        a = np.zeros((c, c), np.float32)
        for r in range(c):
            if upper[r]:
                a[r, m[r]:r + 1] = 1.0
            else:
                a[r, r + 1:m[r]] = 1.0
        if 1 <= l < HG_SMALL_LEVELS:
            blocks.append(a)
        same = (t[:, None] >> (l + 1)) == (t[None, :] >> (l + 1))
        masks.append((same & upper[:, None] & (~upper)[None, :]).astype(np.float32))
        qrows.append(np.broadcast_to(upper[:, None], (c, D_HEAD)).astype(np.float32))
    blocks.append(np.ones((16, c), np.float32))
    a_f = np.concatenate(blocks, axis=0)
    m_f = np.stack(masks)
    q_f = np.stack(qrows)
    a_b = np.concatenate([b[::-1, ::-1] for b in blocks], axis=0)
    m_b = m_f[:, ::-1, ::-1]
    q_b = q_f[:, ::-1, :]
    return (np.stack([a_f, a_b]), np.stack([m_f, m_b]).copy(), np.stack([q_f, q_b]).copy(), nl)


def _hgrn_kernel(q_ref, f_ref, v_ref, lb_ref, a_ref, m_ref, qr_ref, o_ref, st_ref, *, c, nck, nl, scale, hpg):
    d = pl.program_id(0)
    i = pl.program_id(3)

    @pl.when(i == 0)
    def _():
        st_ref[...] = jnp.zeros_like(st_ref)

    probs = [(j, g) for j in range(nck) for g in range(hpg)]

    def rows(j):
        jj = jnp.where(d == 0, j, nck - 1 - j)
        return pl.ds(pl.multiple_of(jj * c, c), c)

    def cols(g):
        return slice(g * D_HEAD, (g + 1) * D_HEAD)

    def eblk(k, g2, rows=c):
        e2 = _dot(a_ref[pl.ds(k * c, rows), :], g2)
        return e2[:, :D_HEAD] + e2[:, D_HEAD:]

    def boundary(gcum, l):
        cc = 1 << l
        pieces = []
        for p in range(c // (2 * cc)):
            m = p * 2 * cc + cc
            rowv = jnp.where(d == 0, gcum[m - 1:m, :], gcum[m:m + 1, :])
            pieces.append(jnp.broadcast_to(rowv, (2 * cc, D_HEAD)))
        return pieces[0] if len(pieces) == 1 else jnp.concatenate(pieces, axis=0)

    qs, kk, ff, vb, vv, g2, gcum, gtot = {}, {}, {}, {}, {}, {}, {}, {}
    for (j, g) in probs:
        lb = lb_ref[g, 0:1, :]
        one_m_lb = 1.0 - lb
        fr = f_ref[rows(j), cols(g)]
        qs[j, g] = q_ref[rows(j), cols(g)] * scale
        v = v_ref[rows(j), cols(g)]
        vv[j, g] = v
        vb[j, g] = v.astype(BF16)
        sig = _sigmoid(fr)
        ff[j, g] = lb + one_m_lb * sig
        kk[j, g] = one_m_lb * (1.0 - sig)
        gl = jnp.log(ff[j, g]) * LOG2E
        g_hi, g_mid = _split2(gl)
        g2[j, g] = jnp.concatenate([g_hi, g_mid], axis=1)
    for p in probs:
        gcum[p] = eblk(0, g2[p])
        gtot[p] = eblk(HG_SMALL_LEVELS, g2[p], rows=16)[0:1, :]

    s = {}
    for l in range(nl):
        if l == 0:
            wq = {p: (qs[p] * ff[p]).astype(BF16) for p in probs}
            wk = {p: kk[p].astype(BF16) for p in probs}
        else:
            if l < HG_SMALL_LEVELS:
                z = {p: jnp.exp2(eblk(l, g2[p])) for p in probs}
            else:
                sgn = 2.0 * qr_ref[l] - 1.0
                z = {p: jnp.exp2((gcum[p] - boundary(gcum[p], l)) * sgn) for p in probs}
            wq = {p: (qs[p] * z[p]).astype(BF16) for p in probs}
            wk = {p: (kk[p] * z[p]).astype(BF16) for p in probs}
        sd = {p: _dot_nt(wq[p], wk[p]) for p in probs}
        for p in probs:
            s[p] = m_ref[l] * sd[p] if l == 0 else s[p] + m_ref[l] * sd[p]

    o_in, qd, upd, dec = {}, {}, {}, {}
    for p in probs:
        dg = jnp.sum(qs[p] * kk[p], axis=-1, keepdims=True)
        o_in[p] = _dot(s[p].astype(BF16), vb[p]) + dg * vv[p]
        qd[p] = (qs[p] * jnp.exp2(gcum[p])).astype(BF16)
        kd = (kk[p] * jnp.exp2(gtot[p] - gcum[p])).astype(BF16)
        upd[p] = _dot_tn(vb[p], kd)
        dec[p] = jnp.exp2(gtot[p])

    st = [st_ref[g] for g in range(hpg)]
    for j in range(nck):
        for g in range(hpg):
            p = (j, g)
            o_ref[rows(j), cols(g)] = (o_in[p] + _dot_nt(qd[p], st[g].astype(BF16))).astype(o_ref.dtype)
            st[g] = st[g] * dec[p] + upd[p]
    for g in range(hpg):
        st_ref[g] = st[g]


def _hgrn2(proj, lb2, nb, t, tbk, hpg):
    c = HG_CHUNK
    n = nb * t
    nblk = t // tbk
    a_np, m_np, q_np, nl = _hgrn_constants(c)
    a_all = jnp.asarray(a_np, BF16)
    masks = jnp.asarray(m_np, F32)
    qrows = jnp.asarray(q_np, F32)
    lb8 = jnp.broadcast_to(lb2.reshape(2, N_HEADS, 1, D_HEAD), (2, N_HEADS, SUBLANES, D_HEAD))
    ngrp = N_HEADS // hpg
    wb = hpg * D_HEAD

    def row(d, b, i):
        return b * nblk + jnp.where(d == 0, i, nblk - 1 - i)

    kern = functools.partial(_hgrn_kernel, c=c, nck=tbk // c, nl=nl, scale=D_HEAD ** -0.5, hpg=hpg)
    na = a_np.shape[1]
    return pl.pallas_call(
        kern,
        grid=(2, nb, ngrp, nblk),
        in_specs=[
            pl.BlockSpec((tbk, wb), lambda d, b, h, i: (row(d, b, i), h)),
            pl.BlockSpec((tbk, wb), lambda d, b, h, i: (row(d, b, i), (1 + d) * ngrp + h)),
            pl.BlockSpec((tbk, wb), lambda d, b, h, i: (row(d, b, i), 3 * ngrp + h)),
            pl.BlockSpec((None, hpg, SUBLANES, D_HEAD), lambda d, b, h, i: (d, h, 0, 0)),
            pl.BlockSpec((None, na, c), lambda d, b, h, i: (d, 0, 0)),
            pl.BlockSpec((None, nl, c, c), lambda d, b, h, i: (d, 0, 0, 0)),
            pl.BlockSpec((None, nl, c, D_HEAD), lambda d, b, h, i: (d, 0, 0, 0)),
        ],
        out_specs=pl.BlockSpec((None, tbk, wb), lambda d, b, h, i: (d, row(d, b, i), h)),
        out_shape=jax.ShapeDtypeStruct((2, n, SEG), BF16),
        scratch_shapes=[pltpu.VMEM((hpg, D_HEAD, D_HEAD), F32)],
        compiler_params=_cparams(("parallel", "parallel", "parallel", "arbitrary")),
        name="hgrn2_bidir",
    )(proj, proj, proj, lb8, a_all, masks, qrows)


def _gdn_prep_kernel(prev_ref, cur_ref, nxt_ref, w_ref, o_ref, *, tbk, scale):
    i = pl.program_id(1)
    s = pl.program_id(2)
    nblk = pl.num_programs(1)
    prev = jnp.where(i == 0, 0.0, prev_ref[...])
    nxt = jnp.where(i == nblk - 1, 0.0, nxt_ref[...])
    full = jnp.concatenate([prev, cur_ref[...], nxt], axis=0)
    rows = tbk + 2 * SUBLANES
    acc = jnp.zeros((tbk, SEG), F32)
    for j in range(CONV_K):
        shift = (CONV_K // 2 - j) % rows
        sh = full if shift == 0 else pltpu.roll(full, shift, axis=0)
        acc = acc + sh[SUBLANES:SUBLANES + tbk, :] * w_ref[j:j + 1, :]
    y = acc * _sigmoid(acc)
    fac_all = jnp.where(s == 0, scale, 1.0)
    for h in range(N_HEADS):
        ys = y[:, h * D_HEAD:(h + 1) * D_HEAD]
        ss = jnp.sum(ys * ys, axis=-1, keepdims=True)
        fac = jnp.where(s < 2, lax.rsqrt(ss + NORM_EPS), 1.0) * fac_all
        o_ref[:, h * D_HEAD:(h + 1) * D_HEAD] = ys * fac


def _gdn_prep(proj, conv_w, nb, t, tbk):
    n = nb * t
    nblk = t // tbk
    hb = tbk // SUBLANES
    w8 = jnp.concatenate([conv_w, jnp.zeros((SUBLANES - CONV_K, 3 * SEG), F32)], axis=0)
    last = n // SUBLANES - 1
    kern = functools.partial(_gdn_prep_kernel, tbk=tbk, scale=D_HEAD ** -0.5)
    return pl.pallas_call(
        kern,
        grid=(nb, nblk, 3),
        in_specs=[
            pl.BlockSpec((SUBLANES, SEG), lambda b, i, s: (jnp.maximum((b * nblk + i) * hb - 1, 0), 5 + s)),
            pl.BlockSpec((tbk, SEG), lambda b, i, s: (b * nblk + i, 5 + s)),
            pl.BlockSpec((SUBLANES, SEG), lambda b, i, s: (jnp.minimum((b * nblk + i + 1) * hb, last), 5 + s)),
            pl.BlockSpec((SUBLANES, SEG), lambda b, i, s: (0, s)),
        ],
        out_specs=pl.BlockSpec((None, tbk, SEG), lambda b, i, s: (s, b * nblk + i, 0)),
        out_shape=jax.ShapeDtypeStruct((3, n, SEG), F32),
        compiler_params=_cparams(("parallel", "parallel", "arbitrary")),
        name="gdn_conv_prep",
    )(proj, proj, proj, w8)


def _gdn_constants(c):
    nl = int(math.log2(c))
    t = np.arange(c)
    incl = (t[None, :] <= t[:, None]).astype(np.float32)
    strict = (t[None, :] < t[:, None]).astype(np.float32)
    after = (t[:, None] > t[None, :]).astype(np.float32)
    rhs_f = np.concatenate([np.ones((c, LANES), np.float32), after], axis=1)
    lvl = []
    for l in range(nl):
        upper = ((t >> l) & 1) == 1
        same = (t[:, None] >> (l + 1)) == (t[None, :] >> (l + 1))
        lvl.append((same & upper[:, None] & (~upper)[None, :]).astype(np.float32))
    lvl = np.stack(lvl)
    mk_f = np.concatenate([incl[None], strict[None], lvl], axis=0)
    mk_b = mk_f[:, ::-1, ::-1]
    rhs_b = np.concatenate([np.ones((c, LANES), np.float32), after[::-1, ::-1]], axis=1)
    return np.stack([mk_f, mk_b]).copy(), np.stack([rhs_f, rhs_b]).copy(), nl


def _gdn_kernel(coef_ref, q_ref, k_ref, v_ref, arow_ref, ps_ref, mk_ref, rhs_ref, o_ref, st_ref, *, c, nck, nl, hpg):
    d = pl.program_id(0)
    hg = pl.program_id(2)
    i = pl.program_id(3)

    @pl.when(i == 0)
    def _():
        st_ref[...] = jnp.zeros_like(st_ref)

    lane = lax.broadcasted_iota(jnp.int32, (c, LANES), 1)
    incl = mk_ref[0]
    strict = mk_ref[1]
    eye = incl - strict
    incl_b = incl.astype(BF16)
    rhs = rhs_ref[...]
    probs = [(j, g) for j in range(nck) for g in range(hpg)]

    def rows(j):
        jj = jnp.where(d == 0, j, nck - 1 - j)
        return pl.ds(pl.multiple_of(jj * c, c), c)

    def cols(g):
        return slice(g * D_HEAD, (g + 1) * D_HEAD)

    qn, kn, vv, beta, gcum, dec, etot = {}, {}, {}, {}, {}, {}, {}
    for (j, g) in probs:
        h = hg * hpg + g
        neg_ea = coef_ref[0, d * N_HEADS + h]
        dtb = coef_ref[1, d * N_HEADS + h]
        a_blk = arow_ref[pl.ds(h, 1), :]
        a_row = jnp.where(d == 0, a_blk[:, j * c:(j + 1) * c], a_blk[:, (nck - 1 - j) * c:(nck - j) * c])
        z = a_row + dtb
        g_row = neg_ea * (jnp.maximum(z, 0.0) + jnp.log(1.0 + jnp.exp(-jnp.abs(z))))
        beta_lane = 2 * N_HEADS + d * N_HEADS + h
        b_col = jnp.sum(jnp.where(lane == beta_lane, ps_ref[rows(j), :], 0.0), axis=-1, keepdims=True)
        beta[j, g] = _sigmoid(b_col)
        g_hi, g_mid = _split2(g_row)
        out = _dot(jnp.concatenate([incl_b * g_hi, incl_b * g_mid], axis=0), rhs)
        out = out[:c] + out[c:]
        gcum[j, g] = out[:, :LANES]
        dmat = out[:, LANES:]
        dec[j, g] = jnp.exp2(jnp.minimum(dmat, 0.0)) * incl
        qn[j, g] = q_ref[rows(j), cols(g)]
        kn[j, g] = k_ref[rows(j), cols(g)]
        vv[j, g] = v_ref[rows(j), cols(g)]

    kb, knb, a_dec, tinv = {}, {}, {}, {}
    for p in probs:
        kb[p] = kn[p] * beta[p]
        knb[p] = kn[p].astype(BF16)
        a_dec[p] = _dot_nt(kb[p].astype(BF16), knb[p]) * dec[p]
        tinv[p] = eye - a_dec[p] * mk_ref[2]
    a_b = {p: a_dec[p].astype(BF16) for p in probs}
    for l in range(1, nl):
        ml_b = mk_ref[2 + l].astype(BF16)
        tb = {p: tinv[p].astype(BF16) for p in probs}
        ta = {p: _dot(tb[p], a_b[p] * ml_b).astype(BF16) for p in probs}
        for p in probs:
            tinv[p] = tinv[p] - _dot(ta[p], tb[p])

    u, w, qk, qe, ke = {}, {}, {}, {}, {}
    for p in probs:
        gtot = jnp.where(d == 0, gcum[p][c - 1:c, :], gcum[p][0:1, :])
        eg = jnp.exp2(gcum[p])
        etot[p] = jnp.exp2(gtot)
        uw = _dot(tinv[p].astype(BF16), jnp.concatenate([vv[p] * beta[p], kb[p] * eg], axis=1).astype(BF16))
        u[p] = uw[:, :D_HEAD]
        w[p] = uw[:, D_HEAD:].astype(BF16)
        qk[p] = (_dot_nt(qn[p].astype(BF16), knb[p]) * dec[p]).astype(BF16)
        qe[p] = (qn[p] * eg).astype(BF16)
        ke[p] = (kn[p] * jnp.exp2(gtot - gcum[p])).astype(BF16)

    st = [st_ref[g] for g in range(hpg)]
    for j in range(nck):
        stb = [st[g].astype(BF16) for g in range(hpg)]
        ws = [_dot(w[j, g], stb[g]) for g in range(hpg)]
        qs = [_dot(qe[j, g], stb[g]) for g in range(hpg)]
        vnb = [(u[j, g] - ws[g]).astype(BF16) for g in range(hpg)]
        for g in range(hpg):
            o_ref[rows(j), cols(g)] = (qs[g] + _dot(qk[j, g], vnb[g])).astype(o_ref.dtype)
        for g in range(hpg):
            st[g] = st[g] * etot[j, g] + _dot_tn(ke[j, g], vnb[g])
    for g in range(hpg):
        st_ref[g] = st[g]


def _gdn(qkv, ps, ps_t, coef, nb, t, tbk, hpg):
    c = GD_CHUNK
    n = nb * t
    nblk = t // tbk
    mk_np, rhs_np, nl = _gdn_constants(c)
    mk = jnp.asarray(mk_np, F32)
    rhs = jnp.asarray(rhs_np, BF16)
    wb = hpg * D_HEAD

    def row(d, b, i):
        return b * nblk + jnp.where(d == 0, i, nblk - 1 - i)

    kern = functools.partial(_gdn_kernel, c=c, nck=tbk // c, nl=nl, hpg=hpg)
    grid_spec = pltpu.PrefetchScalarGridSpec(
        num_scalar_prefetch=0,
        grid=(2, nb, N_HEADS // hpg, nblk),
        in_specs=[
            pl.BlockSpec(memory_space=pltpu.SMEM),
            pl.BlockSpec((None, tbk, wb), lambda d, b, h, i: (0, row(d, b, i), h)),
            pl.BlockSpec((None, tbk, wb), lambda d, b, h, i: (1, row(d, b, i), h)),
            pl.BlockSpec((None, tbk, wb), lambda d, b, h, i: (2, row(d, b, i), h)),
            pl.BlockSpec((SUBLANES, tbk), lambda d, b, h, i: (d, row(d, b, i))),
            pl.BlockSpec((tbk, LANES), lambda d, b, h, i: (row(d, b, i), 0)),
            pl.BlockSpec((None, 2 + nl, c, c), lambda d, b, h, i: (d, 0, 0, 0)),
            pl.BlockSpec((None, c, c + LANES), lambda d, b, h, i: (d, 0, 0)),
        ],
        out_specs=pl.BlockSpec((None, tbk, wb), lambda d, b, h, i: (d, row(d, b, i), h)),
        scratch_shapes=[pltpu.VMEM((hpg, D_HEAD, D_HEAD), F32)],
    )
    return pl.pallas_call(
        kern,
        grid_spec=grid_spec,
        out_shape=jax.ShapeDtypeStruct((2, n, SEG), BF16),
        compiler_params=_cparams(("parallel", "parallel", "parallel", "arbitrary")),
        name="gdn_bidir",
    )(coef, qkv, qkv, qkv, ps_t, ps, mk, rhs)


def _mixout_kernel(oh_ref, og_ref, hg_ref, gz_ref, xa_ref, xb_ref, w_ref, nw_ref, ln_ref, o_ref, lhs_ref, *, na):
    i = pl.program_id(0)

    def gated(o2_ref, gate_ref, wrow, col0):
        for h in range(N_HEADS):
            sl = slice(h * D_HEAD, (h + 1) * D_HEAD)
            o = o2_ref[0, :, sl].astype(F32) + o2_ref[1, :, sl].astype(F32)
            ms = jnp.mean(o * o, axis=-1, keepdims=True)
            gt = gate_ref[:, sl]
            y = o * lax.rsqrt(ms + NORM_EPS) * wrow * (gt * _sigmoid(gt))
            lhs_ref[:, col0 + h * D_HEAD:col0 + (h + 1) * D_HEAD] = y.astype(BF16)

    gated(oh_ref, hg_ref, nw_ref[0:1, :], 0)
    gated(og_ref, gz_ref, nw_ref[1:2, :], SEG)
    hmix = _dot(lhs_ref[...], w_ref[...])
    x = jnp.where(i < na, xa_ref[...], xb_ref[...])
    y = DN_ALPHA * x + hmix
    o_ref[...] = _layernorm_rows(y, ln_ref[0:1, :], ln_ref[1:2, :])


def _mixout(oh, og, proj, xa, xb, w_out_b, nw, ln, tm):
    na = xa.shape[0] // tm
    n = xa.shape[0] + xb.shape[0]
    return pl.pallas_call(
        functools.partial(_mixout_kernel, na=na),
        grid=(n // tm,),
        in_specs=[
            pl.BlockSpec((2, tm, SEG), lambda i: (0, i, 0)),
            pl.BlockSpec((2, tm, SEG), lambda i: (0, i, 0)),
            pl.BlockSpec((tm, SEG), lambda i: (i, 4)),
            pl.BlockSpec((tm, SEG), lambda i: (i, 8)),
            pl.BlockSpec((tm, D_MODEL), lambda i: (jnp.minimum(i, na - 1), 0)),
            pl.BlockSpec((tm, D_MODEL), lambda i: (jnp.maximum(i - na, 0), 0)),
            pl.BlockSpec((2 * SEG, D_MODEL), lambda i: (0, 0)),
            pl.BlockSpec((SUBLANES, D_HEAD), lambda i: (0, 0)),
            pl.BlockSpec((SUBLANES, D_MODEL), lambda i: (0, 0)),
        ],
        out_specs=pl.BlockSpec((tm, D_MODEL), lambda i: (i, 0)),
        out_shape=jax.ShapeDtypeStruct((n, D_MODEL), F32),
        scratch_shapes=[pltpu.VMEM((tm, 2 * SEG), BF16)],
        compiler_params=_cparams(("parallel",)),
        name="mixer_out_ln1",
    )(oh, og, proj, proj, xa, xb, w_out_b, nw, ln)


def _pack_bf16_pairs(x):
    k = x.shape[1] // 2
    lo = lax.bitcast_convert_type(x[:, :k].astype(BF16).astype(F32), jnp.uint32)
    hi = lax.bitcast_convert_type(x[:, k:].astype(BF16).astype(F32), jnp.uint32)
    return (lo >> 16) | (hi & jnp.uint32(0xFFFF0000))


def _unpack_bf16_pairs(w):
    lo = lax.bitcast_convert_type(w << 16, F32).astype(BF16)
    hi = lax.bitcast_convert_type(w & jnp.uint32(0xFFFF0000), F32).astype(BF16)
    return jnp.concatenate([lo, hi], axis=1)


def _attn_kernel(x_ref, kv_ref, wq_ref, wo_ref, ln_ref, wr_ref, br_ref, o_ref, r_ref, xp_ref, att_ref):
    x1 = x_ref[...]
    q = _dot(x1.astype(BF16), wq_ref[...])
    sc = XA_HEAD_DIM ** -0.5
    for h in range(XA_HEADS):
        sl = slice(h * XA_HEAD_DIM, (h + 1) * XA_HEAD_DIM)
        kh = kv_ref[:, sl]
        vh = kv_ref[:, D_MODEL + h * XA_HEAD_DIM:D_MODEL + (h + 1) * XA_HEAD_DIM]
        s = _dot_nt(q[:, sl].astype(BF16), kh) * sc
        s = s - jnp.max(s, axis=-1, keepdims=True)
        p = jnp.exp(s)
        p = p / jnp.sum(p, axis=-1, keepdims=True)
        att_ref[:, sl] = _dot(p.astype(BF16), vh).astype(BF16)
    hx = _dot(att_ref[...], wo_ref[...])
    x2 = _layernorm_rows(DN_ALPHA * x1 + hx, ln_ref[0:1, :], ln_ref[1:2, :])
    o_ref[...] = x2
    xp_ref[...] = _pack_bf16_pairs(x2)

    x_hi, x_mid = _split2(x2)
    lg2 = _dot(x_hi, wr_ref[...])
    lg = lg2[:, :LANES] + lg2[:, LANES:] + _dot(x_mid, wr_ref[:, :LANES]) + br_ref[0:1, :]
    tm = lg.shape[0]
    lane_i = lax.broadcasted_iota(jnp.int32, (tm, LANES), 1)
    lane = lane_i.astype(F32)
    grp_of_lane = (lane_i >> 3).astype(F32)
    neg = jnp.float32(-1e30)
    big = jnp.float32(LANES)
    is_g = (lane_i >= N_EXPERTS) & (lane_i < N_EXPERTS + N_GROUPS)
    gl = jnp.where(is_g, lg, neg)
    gmax = jnp.max(gl, axis=-1, keepdims=True)
    gsel = jnp.min(jnp.where(gl == gmax, lane, big), axis=-1, keepdims=True) - N_EXPERTS
    p_group = 1.0 / jnp.sum(jnp.where(is_g, jnp.exp(gl - gmax), 0.0), axis=-1, keepdims=True)
    in_grp = (lane_i < N_EXPERTS) & (grp_of_lane == gsel)
    el = jnp.where(in_grp, lg, neg)
    m1 = jnp.max(el, axis=-1, keepdims=True)
    i1 = jnp.min(jnp.where(el == m1, lane, big), axis=-1, keepdims=True)
    el2 = jnp.where(lane == i1, neg, el)
    m2 = jnp.max(el2, axis=-1, keepdims=True)
    i2 = jnp.min(jnp.where(el2 == m2, lane, big), axis=-1, keepdims=True)
    e2 = jnp.exp(m2 - m1)
    g1 = p_group / (1.0 + e2)
    g2 = p_group * e2 / (1.0 + e2)
    r = jnp.where(lane_i == 0, i1, 0.0)
    r = jnp.where(lane_i == 1, i2, r)
    r = jnp.where(lane_i == 2, g1, r)
    r = jnp.where(lane_i == 3, g2, r)
    r_ref[...] = r


def _attn(x1, kv, wq_b, wo_b, ln, wr, br, nb, t, tm):
    n = nb * t
    nblk = t // tm
    n_mem = kv.shape[0] // nb
    return pl.pallas_call(
        _attn_kernel,
        grid=(nb, nblk),
        in_specs=[
            pl.BlockSpec((tm, D_MODEL), lambda b, i: (b * nblk + i, 0)),
            pl.BlockSpec((n_mem, 2 * D_MODEL), lambda b, i: (b, 0)),
            pl.BlockSpec((D_MODEL, D_MODEL), lambda b, i: (0, 0), pipeline_mode=pl.Buffered(1)),
            pl.BlockSpec((D_MODEL, D_MODEL), lambda b, i: (0, 0), pipeline_mode=pl.Buffered(1)),
            pl.BlockSpec((SUBLANES, D_MODEL), lambda b, i: (0, 0)),
            pl.BlockSpec((D_MODEL, 2 * LANES), lambda b, i: (0, 0), pipeline_mode=pl.Buffered(1)),
            pl.BlockSpec((SUBLANES, LANES), lambda b, i: (0, 0)),
        ],
        out_specs=[pl.BlockSpec((tm, D_MODEL), lambda b, i: (b * nblk + i, 0)),
                   pl.BlockSpec((tm, LANES), lambda b, i: (b * nblk + i, 0)),
                   pl.BlockSpec((tm, D_MODEL // 2), lambda b, i: (b * nblk + i, 0))],
        out_shape=[jax.ShapeDtypeStruct((n, D_MODEL), F32), jax.ShapeDtypeStruct((n, LANES), F32),
                   jax.ShapeDtypeStruct((n, D_MODEL // 2), jnp.uint32)],
        scratch_shapes=[pltpu.VMEM((tm, D_MODEL), BF16)],
        compiler_params=_cparams(("parallel", "arbitrary")),
        name="xattn_ln2_router",
    )(x1, kv, wq_b, wo_b, ln, wr, br)


DMA_UNROLL = 8


def _issue_row_gather(idx_ref, src_hbm, dst, sem, n_rows):
    def body(k, carry):
        for u in range(DMA_UNROLL):
            r = k * DMA_UNROLL + u
            pltpu.make_async_copy(src_hbm.at[pl.ds(idx_ref[0, r], 1)], dst.at[pl.ds(r, 1)], sem).start(priority=u % 2)
        return carry

    lax.fori_loop(0, n_rows // DMA_UNROLL, body, 0)


def _issue_row_gather_inline(idx_ref, src_hbm, dst, sem, n_rows):
    for r in range(n_rows):
        pltpu.make_async_copy(src_hbm.at[pl.ds(idx_ref[0, r], 1)], dst.at[pl.ds(r, 1)], sem).start(priority=r % 2)


def _wait_row_gather(src_hbm, dst, sem, n_rows):
    pltpu.make_async_copy(src_hbm.at[pl.ds(0, n_rows)], dst, sem).wait()


def _moe_kernel(be_ref, nv_ref, tok_ref, tokn_ref, x_hbm, wg_ref, wu_ref, wd_ref, o_ref, xbuf, sem,
                wgb, wub, wdb, *, tb):
    b = pl.program_id(0)
    nb = pl.num_programs(0)
    slot = b % 2

    @pl.when((b == 0) | (be_ref[b] != be_ref[jnp.maximum(b - 1, 0)]))
    def _():
        wgb[...] = wg_ref[...].astype(BF16)
        wub[...] = wu_ref[...].astype(BF16)
        wdb[...] = wd_ref[...].astype(BF16)

    @pl.when((b == 0) & (nv_ref[0] > 0))
    def _():
        _issue_row_gather(tok_ref, x_hbm, xbuf.at[0], sem.at[0], tb)

    has_next = (b + 1 < nb) & (nv_ref[jnp.minimum(b + 1, nb - 1)] > 0)

    def compute():
        xb = _unpack_bf16_pairs(xbuf[slot])
        h1 = _dot(xb, wgb[...])
        h2 = _dot(xb, wub[...])
        hdn = (h1 * _sigmoid(h1) * h2).astype(BF16)
        o_ref[...] = _dot(hdn, wdb[...])

    @pl.when((nv_ref[b] > 0) & has_next)
    def _():
        _wait_row_gather(x_hbm, xbuf.at[slot], sem.at[slot], tb)
        compute()
        _issue_row_gather_inline(tokn_ref, x_hbm, xbuf.at[1 - slot], sem.at[1 - slot], tb)

    @pl.when((nv_ref[b] > 0) & jnp.logical_not(has_next))
    def _():
        _wait_row_gather(x_hbm, xbuf.at[slot], sem.at[slot], tb)
        compute()

    @pl.when((nv_ref[b] == 0) & has_next)
    def _():
        _issue_row_gather(tokn_ref, x_hbm, xbuf.at[1 - slot], sem.at[1 - slot], tb)

    @pl.when(nv_ref[b] == 0)
    def _():
        o_ref[...] = jnp.zeros_like(o_ref)


def _moe(x2, block_e, block_nv, slot_tok, wg_b, wu_b, wd_b):
    tb = MOE_TB
    cap = slot_tok.shape[0]
    n_blocks = cap // tb
    tok3 = slot_tok.reshape(n_blocks, 1, tb)
    grid_spec = pltpu.PrefetchScalarGridSpec(
        num_scalar_prefetch=2,
        grid=(n_blocks,),
        in_specs=[
            pl.BlockSpec((None, 1, tb), lambda b, be, nv: (b, 0, 0), memory_space=pltpu.SMEM),
            pl.BlockSpec((None, 1, tb), lambda b, be, nv: (jnp.minimum(b + 1, n_blocks - 1), 0, 0),
                         memory_space=pltpu.SMEM),
            pl.BlockSpec(memory_space=pl.ANY),
            pl.BlockSpec((None, D_MODEL, D_EXPERT), lambda b, be, nv: (be[b], 0, 0)),
            pl.BlockSpec((None, D_MODEL, D_EXPERT), lambda b, be, nv: (be[b], 0, 0)),
            pl.BlockSpec((None, D_EXPERT, D_MODEL), lambda b, be, nv: (be[b], 0, 0)),
        ],
        out_specs=pl.BlockSpec((tb, D_MODEL), lambda b, be, nv: (b, 0)),
        scratch_shapes=[pltpu.VMEM((2, tb, D_MODEL // 2), jnp.uint32), pltpu.SemaphoreType.DMA((2,)),
                        pltpu.VMEM((D_MODEL, D_EXPERT), BF16), pltpu.VMEM((D_MODEL, D_EXPERT), BF16),
                        pltpu.VMEM((D_EXPERT, D_MODEL), BF16)],
    )
    return pl.pallas_call(
        functools.partial(_moe_kernel, tb=tb),
        grid_spec=grid_spec,
        out_shape=jax.ShapeDtypeStruct((cap, D_MODEL), F32),
        compiler_params=_cparams(("arbitrary",)),
        name="moe_grouped_mlp",
    )(block_e, block_nv, tok3, tok3, x2, wg_b, wu_b, wd_b)


def _combine_kernel(pos_ref, posn_ref, y_hbm, x_ref, r_ref, ln_ref, op_ref, os_ref, ybuf, sem, *, tm, npt):
    i = pl.program_id(0)
    nt = pl.num_programs(0)
    slot = i % 2

    @pl.when(i == 0)
    def _():
        _issue_row_gather(pos_ref, y_hbm, ybuf.at[0], sem.at[0], 2 * tm)

    def finish(issue_next):
        r = r_ref[...]
        y = r[:, 2:3] * ybuf[slot, 0:tm, :] + r[:, 3:4] * ybuf[slot, tm:2 * tm, :]
        res = _layernorm_rows(DN_ALPHA * x_ref[...] + y, ln_ref[0:1, :], ln_ref[1:2, :])
        if issue_next:
            _issue_row_gather_inline(posn_ref, y_hbm, ybuf.at[1 - slot], sem.at[1 - slot], 2 * tm)

        @pl.when(i < npt)
        def _():
            op_ref[...] = res

        @pl.when(i >= npt)
        def _():
            os_ref[...] = res

    @pl.when(i + 1 < nt)
    def _():
        _wait_row_gather(y_hbm, ybuf.at[slot], sem.at[slot], 2 * tm)
        finish(True)

    @pl.when(i + 1 >= nt)
    def _():
        _wait_row_gather(y_hbm, ybuf.at[slot], sem.at[slot], 2 * tm)
        finish(False)


def _combine(y_slots, pos, x2, r, ln, tm, n_first):
    n = x2.shape[0]
    nt = n // tm
    npt = n_first // tm
    grid_spec = pltpu.PrefetchScalarGridSpec(
        num_scalar_prefetch=0,
        grid=(nt,),
        in_specs=[
            pl.BlockSpec((None, 1, 2 * tm), lambda i: (i, 0, 0), memory_space=pltpu.SMEM),
            pl.BlockSpec((None, 1, 2 * tm), lambda i: (jnp.minimum(i + 1, nt - 1), 0, 0), memory_space=pltpu.SMEM),
            pl.BlockSpec(memory_space=pl.ANY),
            pl.BlockSpec((tm, D_MODEL), lambda i: (i, 0)),
            pl.BlockSpec((tm, LANES), lambda i: (i, 0)),
            pl.BlockSpec((SUBLANES, D_MODEL), lambda i: (0, 0)),
        ],
        out_specs=[pl.BlockSpec((tm, D_MODEL), lambda i: (jnp.minimum(i, npt - 1), 0)),
                   pl.BlockSpec((tm, D_MODEL), lambda i: (jnp.maximum(i - npt, 0), 0))],
        scratch_shapes=[pltpu.VMEM((2, 2 * tm, D_MODEL), F32), pltpu.SemaphoreType.DMA((2,))],
    )
    return pl.pallas_call(
        functools.partial(_combine_kernel, tm=tm, npt=npt),
        grid_spec=grid_spec,
        out_shape=[jax.ShapeDtypeStruct((n_first, D_MODEL), F32), jax.ShapeDtypeStruct((n - n_first, D_MODEL), F32)],
        compiler_params=_cparams(("arbitrary",)),
        name="moe_combine_ln3",
    )(pos, pos, y_slots, x2, r, ln)


def _route_slots(e1, e2, tb, tm):
    n = e1.shape[0]
    flat_e = jnp.stack([e1, e2], axis=1).reshape(-1)
    n_assign = 2 * n
    n_blocks = -(-n_assign // tb) + N_EXPERTS
    cap = n_blocks * tb
    experts = jnp.arange(N_EXPERTS, dtype=jnp.int32)
    counts = jnp.sum((flat_e[:, None] == experts[None, :]).astype(jnp.int32), axis=0)
    padded = (counts + tb - 1) // tb * tb
    pad_end = jnp.cumsum(padded)
    need_end = jnp.cumsum(padded - counts)
    fill = jnp.arange(cap - n_assign, dtype=jnp.int32)
    fill_e = jnp.sum((need_end[None, :] <= fill[:, None]).astype(jnp.int32), axis=1)
    keys = jnp.concatenate([flat_e, fill_e])
    a_id = jnp.concatenate([jnp.arange(n_assign, dtype=jnp.int32),
                            n_assign + jnp.arange(cap - n_assign, dtype=jnp.int32)])
    _, slot_a = lax.sort((keys, a_id), num_keys=1, is_stable=True)
    slot_tok = jnp.where(slot_a < n_assign, slot_a // 2, 0).astype(jnp.int32)
    _, slot_of = lax.sort((slot_a, jnp.arange(cap, dtype=jnp.int32)), num_keys=1)
    slot_of = slot_of[:n_assign]
    starts = jnp.arange(n_blocks, dtype=jnp.int32) * tb
    block_e = jnp.minimum(jnp.sum((pad_end[None, :] <= starts[:, None]).astype(jnp.int32), axis=1),
                          N_EXPERTS - 1).astype(jnp.int32)
    block_nv = jnp.sum((slot_a.reshape(n_blocks, tb) < n_assign).astype(jnp.int32), axis=1)
    pos = slot_of.reshape(n // tm, tm, 2).transpose(0, 2, 1).reshape(n // tm, 1, 2 * tm)
    return block_e, block_nv, slot_tok, pos


def _pad_rows(a, rows=SUBLANES):
    return jnp.concatenate([a, jnp.zeros((rows - a.shape[0],) + a.shape[1:], a.dtype)], axis=0)


def _layer(x_first, x_rest, mem, w_in, hgrn_lb, hgrn_norm_w, gdn_conv_w, gdn_a_log, gdn_dt_bias, gdn_norm_w, w_out,
           ln1_g, ln1_b, xa_w_q, xa_w_kv, xa_w_o, ln2_g, ln2_b, moe_w_group, moe_b_group,
           moe_w_expert, moe_b_expert, moe_w_gate, moe_w_up, moe_w_down, ln3_g, ln3_b,
           *, tm_mm=1024, tbk=512, tm_row=256):
    nb_first, t, _ = x_first.shape
    nb = nb_first + x_rest.shape[0]
    xa = x_first.reshape(nb_first * t, D_MODEL)
    xb = x_rest.reshape((nb - nb_first) * t, D_MODEL)
    layer = 0

    w_in_b = w_in[layer].astype(BF16)
    n_main = 9 * SEG
    w_small = jnp.concatenate([w_in_b[:, n_main:], jnp.zeros((D_MODEL, LANES - 4 * N_HEADS), BF16)], axis=1)
    proj, ps = _in_proj(xa, xb, w_in_b[:, :n_main], w_small, tm_mm, SEG)

    lb2 = jnp.cumsum(jax.nn.softmax(hgrn_lb.astype(F32), axis=0), axis=0)[layer]
    oh = _hgrn2(proj, lb2, nb, t, HG_TBK, HG_HPG)

    qkv = _gdn_prep(proj, gdn_conv_w[layer, :, 0, :].astype(F32), nb, t, tbk)
    coef = jnp.stack([-jnp.exp(gdn_a_log[layer].astype(F32)).reshape(-1) * LOG2E,
                      gdn_dt_bias[layer].astype(F32).reshape(-1)], axis=0)
    og = _gdn(qkv, ps, ps.T, coef, nb, t, GD_TBK, GD_HPG)

    nw = _pad_rows(jnp.stack([hgrn_norm_w[layer], gdn_norm_w[layer]], axis=0).astype(F32))
    ln1 = _pad_rows(jnp.stack([ln1_g[layer], ln1_b[layer]], axis=0).astype(F32))
    x1 = _mixout(oh, og, proj, xa, xb, w_out[layer].astype(BF16), nw, ln1, tm_row)

    n_mem = mem.shape[1]
    kv = _matmul(mem.reshape(nb * n_mem, D_MODEL).astype(BF16), xa_w_kv[layer].astype(BF16),
                 n_mem, 1024, BF16)
    ln2 = _pad_rows(jnp.stack([ln2_g[layer], ln2_b[layer]], axis=0).astype(F32))
    w_r = jnp.concatenate([moe_w_expert[layer], moe_w_group[layer],
                           jnp.zeros((D_MODEL, LANES - N_EXPERTS - N_GROUPS), F32)], axis=1).astype(F32)
    w_r_hi = w_r.astype(BF16)
    w_r_mid = (w_r - w_r_hi.astype(F32)).astype(BF16)
    wr = jnp.concatenate([w_r_hi, w_r_mid], axis=1)
    b_r = jnp.concatenate([moe_b_expert[layer], moe_b_group[layer],
                           jnp.zeros((LANES - N_EXPERTS - N_GROUPS,), F32)]).astype(F32)
    br = jnp.broadcast_to(b_r[None, :], (SUBLANES, LANES))
    x2, r, x2p = _attn(x1, kv, xa_w_q[layer].astype(BF16), xa_w_o[layer].astype(BF16), ln2, wr, br, nb, t,
                       min(ATT_TM, t))

    e1 = r[:, 0].astype(jnp.int32)
    e2 = r[:, 1].astype(jnp.int32)
    block_e, block_nv, slot_tok, pos = _route_slots(e1, e2, MOE_TB, tm_row)
    y_slots = _moe(x2p, block_e, block_nv, slot_tok, moe_w_gate[layer].astype(F32),
                   moe_w_up[layer].astype(F32), moe_w_down[layer].astype(F32))

    ln3 = _pad_rows(jnp.stack([ln3_g[layer], ln3_b[layer]], axis=0).astype(F32))
    y_first, y_rest = _combine(y_slots, pos, x2, r, ln3, tm_row, nb_first * t)
    return y_first.reshape(nb_first, t, D_MODEL), y_rest.reshape(nb - nb_first, t, D_MODEL)


def kernel(x_prompt, x_sample, mem_prompt, mem_sample, w_in, hgrn_lb, hgrn_norm_w, gdn_conv_w, gdn_a_log, gdn_dt_bias, gdn_norm_w, w_out, ln1_g, ln1_b, xa_w_q, xa_w_kv, xa_w_o, ln2_g, ln2_b, moe_w_group, moe_b_group, moe_w_expert, moe_b_expert, moe_w_gate, moe_w_up, moe_w_down, ln3_g, ln3_b):
    assert x_prompt.shape[1] == x_sample.shape[1]
    mem = jnp.concatenate([mem_prompt, mem_sample], axis=0)
    return _layer(x_prompt, x_sample, mem, w_in, hgrn_lb, hgrn_norm_w, gdn_conv_w, gdn_a_log, gdn_dt_bias,
                  gdn_norm_w, w_out, ln1_g, ln1_b, xa_w_q, xa_w_kv, xa_w_o, ln2_g, ln2_b, moe_w_group, moe_b_group,
                  moe_w_expert, moe_b_expert, moe_w_gate, moe_w_up, moe_w_down, ln3_g, ln3_b)
```
